```python
import math
import jax, jax.numpy as jnp
from jax import lax
import numpy as np

D_MODEL = 1024
BATCH = 8
SEQ = 2048
DEPTH = 4

N_MIXERS = 3
Q_BLOCK = 128
RMS_EPS = 1e-6
SB_HEADS = 16
SB_HEAD_DIM = D_MODEL // SB_HEADS
S5_GROUP = 16
S5_GROUPS = D_MODEL // S5_GROUP
S5_STATE = 64
S5_DT_MIN = 1e-3
S5_DT_MAX = 1e-1
MLA_HEADS = 16
MLA_NOPE = 64
MLA_ROPE = 32
MLA_V = 64
MLA_Q_RANK = 384
MLA_KV_RANK = 256
ROPE_THETA = 10000.0
D_FF = 2816
N_EXPERTS = 8
TOP_K = 2
D_FF_EXPERT = 1408
N_SB = (DEPTH + 2) // 3
N_S5 = (DEPTH + 1) // 3
N_MLA = DEPTH // 3
N_DENSE = (DEPTH + 1) // 2
N_MOE = DEPTH // 2

kernel_name = 'hybrid_stickbreak_s5_mla_moe_trunk'


def rms_norm(x, g):
    xf = x.astype(jnp.float32)
    y = xf * lax.rsqrt(jnp.mean(xf * xf, axis=-1, keepdims=True) + RMS_EPS)
    return (y * g.astype(jnp.float32)).astype(x.dtype)


def stick_breaking_attention(h, w_qkv, w_o):
    b, l, _ = h.shape
    qkv = (h @ w_qkv).reshape(b, l, 3, SB_HEADS, SB_HEAD_DIM)
    q, k, v = qkv[:, :, 0], qkv[:, :, 1], qkv[:, :, 2]
    scale = SB_HEAD_DIM ** -0.5
    outs = []
    for start in range(0, l, Q_BLOCK):
        end = start + Q_BLOCK
        z = jnp.einsum('bqhd,bkhd->bhqk', q[:, start:end], k[:, :end]).astype(jnp.float32) * scale
        strict = jnp.arange(end)[None, :] < jnp.arange(start, end)[:, None]
        log_one_minus = jnp.where(strict, jax.nn.log_sigmoid(-z), 0.0)
        tail = lax.cumsum(log_one_minus, axis=3, reverse=True) - log_one_minus
        weights = jnp.where(strict, jnp.exp(jax.nn.log_sigmoid(z) + tail), 0.0)
        outs.append(jnp.einsum('bhqk,bkhd->bqhd', weights.astype(v.dtype), v[:, :end]))
    o = jnp.concatenate(outs, axis=1).reshape(b, l, SB_HEADS * SB_HEAD_DIM)
    return o @ w_o


def _complex_affine_combine(prev, nxt):
    a1r, a1i, b1r, b1i = prev
    a2r, a2i, b2r, b2i = nxt
    ar = a2r * a1r - a2i * a1i
    ai = a2r * a1i + a2i * a1r
    br = a2r * b1r - a2i * b1i + b2r
    bi = a2r * b1i + a2i * b1r + b2i
    return (ar, ai, br, bi)


def s5_layer(h, lam_re, lam_im, log_dt, b_re, b_im, c_re, c_im, d_skip, w_glu):
    f32 = jnp.float32
    bsz, l, _ = h.shape
    lam_re = lam_re.astype(f32)
    lam_im = lam_im.astype(f32)
    dt = jnp.exp(log_dt.astype(f32))[:, None]
    mag = jnp.exp(lam_re * dt)
    ang = lam_im * dt
    a_re, a_im = mag * jnp.cos(ang), mag * jnp.sin(ang)
    den = lam_re * lam_re + lam_im * lam_im
    num_re = a_re - 1.0
    f_re = (num_re * lam_re + a_im * lam_im) / den
    f_im = (a_im * lam_re - num_re * lam_im) / den
    b_re = b_re.astype(f32)
    b_im = b_im.astype(f32)
    bb_re = f_re[..., None] * b_re - f_im[..., None] * b_im
    bb_im = f_re[..., None] * b_im + f_im[..., None] * b_re
    u = h.reshape(bsz, l, S5_GROUPS, S5_GROUP).astype(f32)
    bu_re = jnp.einsum('blgc,gpc->lbgp', u, bb_re)
    bu_im = jnp.einsum('blgc,gpc->lbgp', u, bb_im)
    a_re_t = jnp.broadcast_to(a_re, (l, 1) + a_re.shape)
    a_im_t = jnp.broadcast_to(a_im, (l, 1) + a_im.shape)
    _, _, x_re, x_im = lax.associative_scan(_complex_affine_combine, (a_re_t, a_im_t, bu_re, bu_im), axis=0)
    y = (jnp.einsum('lbgp,gcp->blgc', x_re, c_re.astype(f32))
         - jnp.einsum('lbgp,gcp->blgc', x_im, c_im.astype(f32)))
    y = y.reshape(bsz, l, D_MODEL) + d_skip.astype(f32) * h.astype(f32)
    y = jax.nn.gelu(y).astype(h.dtype)
    val, gate = jnp.split(y @ w_glu, 2, axis=-1)
    return val * jax.nn.sigmoid(gate)


def apply_rope(t, cos, sin):
    half = t.shape[-1] // 2
    t1 = t[..., :half].astype(jnp.float32)
    t2 = t[..., half:].astype(jnp.float32)
    return jnp.concatenate([t1 * cos - t2 * sin, t1 * sin + t2 * cos], axis=-1).astype(t.dtype)


def mla_attention(h, w_in, q_norm, w_q_b, kv_norm, w_kv_b, w_o, cos, sin):
    b, l, _ = h.shape
    c = h @ w_in
    q_c = c[..., :MLA_Q_RANK]
    kv_c = c[..., MLA_Q_RANK:MLA_Q_RANK + MLA_KV_RANK]
    k_rope = apply_rope(c[..., MLA_Q_RANK + MLA_KV_RANK:], cos, sin)
    q = (rms_norm(q_c, q_norm) @ w_q_b).reshape(b, l, MLA_HEADS, MLA_NOPE + MLA_ROPE)
    q_nope = q[..., :MLA_NOPE]
    q_rope = apply_rope(q[..., MLA_NOPE:], cos[:, None, :], sin[:, None, :])
    kv = (rms_norm(kv_c, kv_norm) @ w_kv_b).reshape(b, l, MLA_HEADS, MLA_NOPE + MLA_V)
    k_nope, v = kv[..., :MLA_NOPE], kv[..., MLA_NOPE:]
    scale = (MLA_NOPE + MLA_ROPE) ** -0.5
    outs = []
    for start in range(0, l, Q_BLOCK):
        end = start + Q_BLOCK
        s = (jnp.einsum('bqhd,bkhd->bhqk', q_nope[:, start:end], k_nope[:, :end])
             + jnp.einsum('bqhr,bkr->bhqk', q_rope[:, start:end], k_rope[:, :end])).astype(jnp.float32) * scale
        causal = jnp.arange(end)[None, :] <= jnp.arange(start, end)[:, None]
        p = jax.nn.softmax(jnp.where(causal, s, -jnp.inf), axis=-1)
        outs.append(jnp.einsum('bhqk,bkhv->bqhv', p.astype(v.dtype), v[:, :end]))
    o = jnp.concatenate(outs, axis=1).reshape(b, l, MLA_HEADS * MLA_V)
    return o @ w_o


def swiglu(h, w_gate_up, w_down):
    g, u = jnp.split(h @ w_gate_up, 2, axis=-1)
    return (jax.nn.silu(g) * u) @ w_down


def moe_swiglu(h, w_router, b_router, w_gate_up, w_down):
    b, l, d = h.shape
    hf = h.reshape(b * l, d)
    logits = (hf @ w_router).astype(jnp.float32) + b_router.astype(jnp.float32)
    top_v, top_i = lax.top_k(logits, TOP_K)
    top_w = jax.nn.softmax(top_v, axis=-1)
    gates = jnp.sum(jax.nn.one_hot(top_i, N_EXPERTS, dtype=jnp.float32) * top_w[..., None], axis=1)
    out = jnp.zeros((b * l, d), jnp.float32)
    for e in range(N_EXPERTS):
        out = out + gates[:, e:e + 1] * swiglu(hf, w_gate_up[e], w_down[e]).astype(jnp.float32)
    return out.astype(h.dtype).reshape(b, l, d)


def setup_inputs(seed: int = 0) -> dict:
    key = jax.random.key(seed)
    ks = iter(jax.random.split(key, 32))
    f32 = jnp.float32

    def nrm(shape, scale):
        return jax.random.normal(next(ks), shape, f32) * scale

    def gain(shape):
        return 1.0 + 0.05 * jax.random.normal(next(ks), shape, f32)

    state_idx = jnp.arange(S5_STATE, dtype=f32)
    return {
        'x': nrm((BATCH, SEQ, D_MODEL), 1.0),
        'norm_mix': gain((DEPTH, D_MODEL)),
        'norm_ffn': gain((DEPTH, D_MODEL)),
        'final_norm': gain((D_MODEL,)),
        'sb_w_qkv': nrm((N_SB, D_MODEL, 3 * D_MODEL), D_MODEL ** -0.5),
        'sb_w_o': nrm((N_SB, D_MODEL, D_MODEL), D_MODEL ** -0.5),
        's5_lambda_re': -0.5 + 0.01 * jax.random.normal(next(ks), (N_S5, S5_GROUPS, S5_STATE), f32),
        's5_lambda_im': math.pi * state_idx + 0.01 * jax.random.normal(next(ks), (N_S5, S5_GROUPS, S5_STATE), f32),
        's5_log_dt': jax.random.uniform(next(ks), (N_S5, S5_GROUPS), f32, math.log(S5_DT_MIN), math.log(S5_DT_MAX)),
        's5_b_re': nrm((N_S5, S5_GROUPS, S5_STATE, S5_GROUP), (2 * S5_GROUP) ** -0.5),
        's5_b_im': nrm((N_S5, S5_GROUPS, S5_STATE, S5_GROUP), (2 * S5_GROUP) ** -0.5),
        's5_c_re': nrm((N_S5, S5_GROUPS, S5_GROUP, S5_STATE), S5_STATE ** -0.5),
        's5_c_im': nrm((N_S5, S5_GROUPS, S5_GROUP, S5_STATE), S5_STATE ** -0.5),
        's5_d': nrm((N_S5, D_MODEL), 1.0),
        's5_w_glu': nrm((N_S5, D_MODEL, 2 * D_MODEL), D_MODEL ** -0.5),
        'mla_w_in': nrm((N_MLA, D_MODEL, MLA_Q_RANK + MLA_KV_RANK + MLA_ROPE), D_MODEL ** -0.5),
        'mla_q_norm': gain((N_MLA, MLA_Q_RANK)),
        'mla_w_q_b': nrm((N_MLA, MLA_Q_RANK, MLA_HEADS * (MLA_NOPE + MLA_ROPE)), MLA_Q_RANK ** -0.5),
        'mla_kv_norm': gain((N_MLA, MLA_KV_RANK)),
        'mla_w_kv_b': nrm((N_MLA, MLA_KV_RANK, MLA_HEADS * (MLA_NOPE + MLA_V)), MLA_KV_RANK ** -0.5),
        'mla_w_o': nrm((N_MLA, MLA_HEADS * MLA_V, D_MODEL), (MLA_HEADS * MLA_V) ** -0.5),
        'ffn_w_gate_up': nrm((N_DENSE, D_MODEL, 2 * D_FF), D_MODEL ** -0.5),
        'ffn_w_down': nrm((N_DENSE, D_FF, D_MODEL), D_FF ** -0.5),
        'moe_w_router': nrm((N_MOE, D_MODEL, N_EXPERTS), D_MODEL ** -0.5),
        'moe_b_router': nrm((N_MOE, N_EXPERTS), 0.01),
        'moe_w_gate_up': nrm((N_MOE, N_EXPERTS, D_MODEL, 2 * D_FF_EXPERT), D_MODEL ** -0.5),
        'moe_w_down': nrm((N_MOE, N_EXPERTS, D_FF_EXPERT, D_MODEL), D_FF_EXPERT ** -0.5),
    }


def reference(x, norm_mix, norm_ffn, final_norm, sb_w_qkv, sb_w_o,
              s5_lambda_re, s5_lambda_im, s5_log_dt, s5_b_re, s5_b_im, s5_c_re, s5_c_im, s5_d, s5_w_glu,
              mla_w_in, mla_q_norm, mla_w_q_b, mla_kv_norm, mla_w_kv_b, mla_w_o,
              ffn_w_gate_up, ffn_w_down, moe_w_router, moe_b_router, moe_w_gate_up, moe_w_down):
    l = x.shape[1]
    pos = jnp.arange(l, dtype=jnp.float32)
    inv_freq = ROPE_THETA ** (-jnp.arange(0, MLA_ROPE, 2, dtype=jnp.float32) / MLA_ROPE)
    ang = pos[:, None] * inv_freq[None, :]
    cos, sin = jnp.cos(ang), jnp.sin(ang)
    for i in range(DEPTH):
        h = rms_norm(x, norm_mix[i])
        kind, slot = i % N_MIXERS, i // N_MIXERS
        if kind == 0:
            m = stick_breaking_attention(h, sb_w_qkv[slot], sb_w_o[slot])
        elif kind == 1:
            m = s5_layer(h, s5_lambda_re[slot], s5_lambda_im[slot], s5_log_dt[slot], s5_b_re[slot], s5_b_im[slot],
                         s5_c_re[slot], s5_c_im[slot], s5_d[slot], s5_w_glu[slot])
        else:
            m = mla_attention(h, mla_w_in[slot], mla_q_norm[slot], mla_w_q_b[slot], mla_kv_norm[slot],
                              mla_w_kv_b[slot], mla_w_o[slot], cos, sin)
        x = x + m.astype(x.dtype)
        h = rms_norm(x, norm_ffn[i])
        if i % 2 == 0:
            f = swiglu(h, ffn_w_gate_up[i // 2], ffn_w_down[i // 2])
        else:
            f = moe_swiglu(h, moe_w_router[i // 2], moe_b_router[i // 2], moe_w_gate_up[i // 2], moe_w_down[i // 2])
        x = x + f.astype(x.dtype)
    return rms_norm(x, final_norm)
```

```python
import functools
import math

import jax
import jax.numpy as jnp
from jax import lax
from jax.experimental import pallas as pl
from jax.experimental.pallas import tpu as pltpu

F32 = jnp.float32
BF16 = jnp.bfloat16

D_MODEL = 1024
DEPTH = 4
N_MIXERS = 3
RMS_EPS = 1e-6
SB_HEADS = 16
SB_HEAD_DIM = D_MODEL // SB_HEADS
S5_GROUP = 16
S5_GROUPS = D_MODEL // S5_GROUP
S5_STATE = 64
MLA_HEADS = 16
MLA_NOPE = 64
MLA_ROPE = 32
MLA_V = 64
MLA_Q_RANK = 384
MLA_KV_RANK = 256
ROPE_THETA = 10000.0
D_FF = 2816
N_EXPERTS = 8
TOP_K = 2
D_FF_EXPERT = 1408

LANES = 128
SUBLANES = 8
MXU_DIM = 256
VMEM_BYTES_V7X = 64 * 1024 * 1024
VMEM_CAP = VMEM_BYTES_V7X - 8 * 1024 * 1024

ROW_TILE = 512
ATT_BLOCK = MXU_DIM
S5_STEPS = 64
S5_SLAB = MXU_DIM
S5_LANE_CHUNK = 512
FF_CHUNK = 1408


def _params(semantics, est_bytes):
    limit = int(min(VMEM_CAP, max(32 * 1024 * 1024, est_bytes * 5 // 4)))
    return pltpu.CompilerParams(dimension_semantics=semantics, vmem_limit_bytes=limit)


def _rms(xf, g):
    ms = jnp.mean(xf * xf, axis=-1, keepdims=True)
    return xf * lax.rsqrt(ms + RMS_EPS) * g


def _sigmoid(v):
    return 1.0 / (1.0 + jnp.exp(-v))


def _x_spec(x_is_bld, tl, d):
    if x_is_bld:
        return pl.BlockSpec((None, tl, d), lambda i, b: (b, i, 0))
    return pl.BlockSpec((tl, d), lambda i, b: (i, b))


def _norm_proj_blocks_kernel(x_ref, g_ref, w_ref, o_ref):
    h = _rms(x_ref[...], g_ref[...]).astype(BF16)
    nb = o_ref.shape[0]
    per = 4
    for c in range(nb // per):
        r = jnp.dot(h, w_ref[:, c * per * LANES:(c + 1) * per * LANES], preferred_element_type=F32)
        for jj in range(per):
            o_ref[c * per + jj] = r[:, jj * LANES:(jj + 1) * LANES].astype(BF16)


def norm_proj_blocks(x, x_is_bld, bsz, seq, g, w):
    d, n = w.shape
    nb = n // LANES
    tl = min(ROW_TILE, seq)
    est = 2 * tl * d * 4 + 2 * d * n * 2 + 2 * n * tl * 2 + 4 * tl * 512 * 4
    return pl.pallas_call(
        _norm_proj_blocks_kernel,
        grid=(seq // tl, bsz),
        in_specs=[
            _x_spec(x_is_bld, tl, d),
            pl.BlockSpec((1, d), lambda i, b: (0, 0)),
            pl.BlockSpec((d, n), lambda i, b: (0, 0)),
        ],
        out_specs=pl.BlockSpec((nb, None, tl, LANES), lambda i, b: (0, b, i, 0)),
        out_shape=jax.ShapeDtypeStruct((nb, bsz, seq, LANES), BF16),
        compiler_params=_params(("parallel", "parallel"), est),
        name="norm_proj_blocks",
    )(x, g, w)


def _sb_attn_kernel(q_ref, k_ref, v_ref, u_ref, o_ref, *, blk):
    qi = pl.program_id(2)
    q = q_ref[...]
    lane = lax.broadcasted_iota(jnp.int32, (1, LANES), 1)
    row = lax.broadcasted_iota(jnp.int32, (blk, blk), 0)
    col = lax.broadcasted_iota(jnp.int32, (blk, blk), 1)
    strict = col < row
    u = u_ref[...]

    def block(qh, j, carry, mask):
        acc, run = carry
        start = pl.multiple_of(j * blk, blk)
        kb = k_ref[pl.ds(start, blk), :]
        vb = v_ref[pl.ds(start, blk), :]
        z = lax.dot_general(qh, kb, (((1,), (1,)), ((), ())), preferred_element_type=F32)
        e = jnp.exp(-jnp.abs(z))
        lom = -(jnp.maximum(z, 0.0) + jnp.log(1.0 + e))
        if mask is not None:
            lom = jnp.where(mask, lom, 0.0)
        tail = jnp.dot(lom.astype(BF16), u, preferred_element_type=F32) + run
        w = jnp.exp(lom + z + tail)
        if mask is not None:
            w = jnp.where(mask, w, 0.0)
        acc = acc + jnp.dot(w.astype(BF16), vb, preferred_element_type=F32)
        run = run + jnp.sum(lom, axis=-1, keepdims=True)
        return acc, run

    accs = []
    for hh in range(2):
        head_lanes = (lane >= SB_HEAD_DIM) if hh else (lane < SB_HEAD_DIM)
        qh = jnp.where(head_lanes, q, jnp.zeros_like(q))
        carry = (jnp.zeros((blk, LANES), F32), jnp.zeros((blk, 1), F32))
        carry = block(qh, qi, carry, strict)
        carry = lax.fori_loop(0, qi, lambda s, c: block(qh, qi - 1 - s, c, None), carry)
        accs.append(carry[0])
    o_ref[...] = jnp.where(lane < SB_HEAD_DIM, accs[0], accs[1]).astype(o_ref.dtype)


def sb_attention(qkv, bsz, seq, tri):
    blk = min(ATT_BLOCK, seq)
    n_pairs = SB_HEADS // 2
    est = 4 * seq * LANES * 2 + 8 * blk * blk * 4 + 4 * blk * LANES * 4
    return pl.pallas_call(
        functools.partial(_sb_attn_kernel, blk=blk),
        grid=(bsz, n_pairs, seq // blk),
        in_specs=[
            pl.BlockSpec((None, None, blk, LANES), lambda b, h, i: (h, b, i, 0)),
            pl.BlockSpec((None, None, seq, LANES), lambda b, h, i: (n_pairs + h, b, 0, 0)),
            pl.BlockSpec((None, None, seq, LANES), lambda b, h, i: (2 * n_pairs + h, b, 0, 0)),
            pl.BlockSpec((blk, blk), lambda b, h, i: (0, 0)),
        ],
        out_specs=pl.BlockSpec((blk, LANES), lambda b, h, i: (i, b * n_pairs + h)),
        out_shape=jax.ShapeDtypeStruct((seq, bsz * D_MODEL), BF16),
        compiler_params=_params(("parallel", "parallel", "arbitrary"), est),
        name="sb_attention",
    )(qkv, qkv, qkv, tri)


def _proj_residual_kernel(a_ref, x_ref, w_ref, o_ref, *, glu):
    r = jnp.dot(a_ref[...], w_ref[...], preferred_element_type=F32)
    if glu:
        n = o_ref.shape[-1]
        r = r[:, :n] * _sigmoid(r[:, n:])
    o_ref[...] = x_ref[...] + r


def proj_residual(a, x, x_is_bld, bsz, seq, w, glu=False):
    k, n_w = w.shape
    n = n_w // 2 if glu else n_w
    tl = min(ROW_TILE, seq)
    est = 2 * tl * k * 2 + 4 * tl * n * 4 + 2 * k * n_w * 2 + 2 * tl * n_w * 4
    return pl.pallas_call(
        functools.partial(_proj_residual_kernel, glu=glu),
        grid=(seq // tl, bsz),
        in_specs=[
            pl.BlockSpec((tl, k), lambda i, b: (i, b)),
            _x_spec(x_is_bld, tl, n),
            pl.BlockSpec((k, n_w), lambda i, b: (0, 0)),
        ],
        out_specs=pl.BlockSpec((tl, n), lambda i, b: (i, b)),
        out_shape=jax.ShapeDtypeStruct((seq, bsz * n), F32),
        compiler_params=_params(("parallel", "parallel"), est),
        name="proj_residual_glu" if glu else "proj_residual",
    )(a, x, w)


def _ffn_kernel(x_ref, g_ref, gates_ref, wg_ref, wu_ref, wd_ref, o_ref, h_ref, *, chunks_per_expert, gated):
    j = pl.program_id(1)

    @pl.when(j == 0)
    def _():
        xf = x_ref[...]
        h_ref[...] = _rms(xf, g_ref[...]).astype(BF16)
        o_ref[...] = xf

    h = h_ref[...]
    gv = jnp.dot(h, wg_ref[...], preferred_element_type=F32)
    uv = jnp.dot(h, wu_ref[...], preferred_element_type=F32)
    act = gv * _sigmoid(gv) * uv
    if gated:
        e = j // chunks_per_expert
        lane = lax.broadcasted_iota(jnp.int32, (1, LANES), 1)
        ge = jnp.sum(jnp.where(lane == e, gates_ref[...], 0.0), axis=-1, keepdims=True)
        act = act * ge
    o_ref[...] += jnp.dot(act.astype(BF16), wd_ref[...], preferred_element_type=F32)


def ffn(x, g, w_gate_up, w_down, gates=None):
    t, d = x.shape
    n_e, f, _ = w_down.shape
    tf = FF_CHUNK
    cpe = f // tf
    tm = min(ROW_TILE, t)
    gated = gates is not None
    if not gated:
        gates = jnp.zeros((SUBLANES, LANES), F32)
    gates_spec = (pl.BlockSpec((tm, LANES), lambda i, j: (i, 0)) if gated
                  else pl.BlockSpec((SUBLANES, LANES), lambda i, j: (0, 0)))
    est = 4 * tm * d * 4 + tm * d * 2 + 6 * d * tf * 2 + 3 * tm * tf * 4
    return pl.pallas_call(
        functools.partial(_ffn_kernel, chunks_per_expert=cpe, gated=gated),
        grid=(t // tm, n_e * cpe),
        in_specs=[
            pl.BlockSpec((tm, d), lambda i, j: (i, 0)),
            pl.BlockSpec((1, d), lambda i, j: (0, 0)),
            gates_spec,
            pl.BlockSpec((None, d, tf), lambda i, j: (j // cpe, 0, j % cpe)),
            pl.BlockSpec((None, d, tf), lambda i, j: (j // cpe, 0, cpe + j % cpe)),
            pl.BlockSpec((None, tf, d), lambda i, j: (j // cpe, j % cpe, 0)),
        ],
        out_specs=pl.BlockSpec((tm, d), lambda i, j: (i, 0)),
        out_shape=jax.ShapeDtypeStruct((t, d), F32),
        scratch_shapes=[pltpu.VMEM((tm, d), BF16)],
        compiler_params=_params(("parallel", "arbitrary"), est),
        name="ffn_gated" if gated else "ffn",
    )(x, g, gates, w_gate_up, w_gate_up, w_down)


def _router_kernel(x_ref, g_ref, w_ref, b_ref, o_ref):
    h = _rms(x_ref[...], g_ref[...])
    logits = jnp.dot(h, w_ref[...], preferred_element_type=F32, precision=lax.Precision.HIGHEST) + b_ref[...]
    lane = lax.broadcasted_iota(jnp.int32, logits.shape, 1).astype(F32)
    neg = jnp.float32(-jnp.inf)
    logits = jnp.where(lane < N_EXPERTS, logits, neg)
    m1 = jnp.max(logits, axis=-1, keepdims=True)
    i1 = jnp.min(jnp.where(logits == m1, lane, float(LANES)), axis=-1, keepdims=True)
    rest = jnp.where(lane == i1, neg, logits)
    m2 = jnp.max(rest, axis=-1, keepdims=True)
    i2 = jnp.min(jnp.where(rest == m2, lane, float(LANES)), axis=-1, keepdims=True)
    e2 = jnp.exp(m2 - m1)
    w1 = 1.0 / (1.0 + e2)
    w2 = e2 * w1
    o_ref[...] = jnp.where(lane == i1, w1, jnp.where(lane == i2, w2, 0.0))


def router(x, g, w_pad, b_pad):
    t, d = x.shape
    tm = min(ROW_TILE, t)
    est = 2 * tm * d * 4 + 2 * d * LANES * 4 + 4 * tm * LANES * 4 + 2 * tm * d * 4
    return pl.pallas_call(
        _router_kernel,
        grid=(t // tm,),
        in_specs=[
            pl.BlockSpec((tm, d), lambda i: (i, 0)),
            pl.BlockSpec((1, d), lambda i: (0, 0)),
            pl.BlockSpec((d, LANES), lambda i: (0, 0)),
            pl.BlockSpec((1, LANES), lambda i: (0, 0)),
        ],
        out_specs=pl.BlockSpec((tm, LANES), lambda i: (i, 0)),
        out_shape=jax.ShapeDtypeStruct((t, LANES), F32),
        compiler_params=_params(("parallel",), est),
        name="router",
    )(x, g, w_pad, b_pad)


def _s5_discretise_kernel(lre_ref, lim_ref, ldt_ref, bre_ref, bim_ref, are_ref, aim_ref, bbre_ref, bbim_ref):
    lam_re = lre_ref[...]
    lam_im = lim_ref[...]
    dt = jnp.exp(ldt_ref[...])
    mag = jnp.exp(lam_re * dt)
    ang = lam_im * dt
    a_re = mag * jnp.cos(ang)
    a_im = mag * jnp.sin(ang)
    den = lam_re * lam_re + lam_im * lam_im
    num_re = a_re - 1.0
    f_re = (num_re * lam_re + a_im * lam_im) / den
    f_im = (a_im * lam_re - num_re * lam_im) / den
    are_ref[...] = a_re
    aim_ref[...] = a_im
    n_slabs, _, width = bre_ref.shape
    for s in range(n_slabs):
        fr = f_re[:, s * width:(s + 1) * width]
        fi = f_im[:, s * width:(s + 1) * width]
        br = bre_ref[s]
        bi = bim_ref[s]
        bbre_ref[s] = (fr * br - fi * bi).astype(BF16)
        bbim_ref[s] = (fr * bi + fi * br).astype(BF16)


def s5_discretise(lam_re_row, lam_im_row, log_dt_row, b_re_blk, b_im_blk):
    n_state = lam_re_row.shape[1]
    shp = b_re_blk.shape
    return pl.pallas_call(
        _s5_discretise_kernel,
        out_shape=(
            jax.ShapeDtypeStruct((1, n_state), F32),
            jax.ShapeDtypeStruct((1, n_state), F32),
            jax.ShapeDtypeStruct(shp, BF16),
            jax.ShapeDtypeStruct(shp, BF16),
        ),
        name="s5_discretise",
    )(lam_re_row, lam_im_row, log_dt_row, b_re_blk, b_im_blk)


def _gelu_tanh(y):
    c = math.sqrt(2.0 / math.pi)
    return 0.5 * y * (1.0 + jnp.tanh(c * (y + 0.044715 * (y * y * y))))


def _s5_kernel(x_ref, g_ref, are_ref, aim_ref, bbre_ref, bbim_ref, cre_ref, cim_ref, d_ref, o_ref,
               sre_ref, sim_ref, stre_ref, stim_ref, h_ref, *, bsz, steps):
    i = pl.program_id(0)

    @pl.when(i == 0)
    def _():
        stre_ref[...] = jnp.zeros_like(stre_ref)
        stim_ref[...] = jnp.zeros_like(stim_ref)

    h_ref[...] = _rms(x_ref[...], g_ref[...])
    n_slabs, slab, width = bbre_ref.shape
    for s in range(n_slabs):
        hs = h_ref[:, s * slab:(s + 1) * slab].astype(BF16)
        sre_ref[:, s * width:(s + 1) * width] = jnp.dot(hs, bbre_ref[s], preferred_element_type=F32)
        sim_ref[:, s * width:(s + 1) * width] = jnp.dot(hs, bbim_ref[s], preferred_element_type=F32)

    n_state = sre_ref.shape[1]
    cw = S5_LANE_CHUNK
    for c in range(n_state // cw):
        lanes = slice(c * cw, (c + 1) * cw)
        ar = jnp.broadcast_to(are_ref[:, lanes], (bsz, cw))
        ai = jnp.broadcast_to(aim_ref[:, lanes], (bsz, cw))

        def step(t, carry):
            xr, xi = carry
            r0 = pl.multiple_of(t * bsz, bsz)
            nr = ar * xr - ai * xi + sre_ref[pl.ds(r0, bsz), lanes]
            ni = ar * xi + ai * xr + sim_ref[pl.ds(r0, bsz), lanes]
            sre_ref[pl.ds(r0, bsz), lanes] = nr
            sim_ref[pl.ds(r0, bsz), lanes] = ni
            return nr, ni

        xr, xi = lax.fori_loop(0, steps, step, (stre_ref[:, lanes], stim_ref[:, lanes]), unroll=8)
        stre_ref[:, lanes] = xr
        stim_ref[:, lanes] = xi

    dsk = d_ref[...]
    for s in range(n_slabs):
        xr = sre_ref[:, s * width:(s + 1) * width].astype(BF16)
        xi = sim_ref[:, s * width:(s + 1) * width].astype(BF16)
        y = (jnp.dot(xr, cre_ref[s], preferred_element_type=F32)
             - jnp.dot(xi, cim_ref[s], preferred_element_type=F32))
        cols = slice(s * slab, (s + 1) * slab)
        y = y + dsk[:, cols] * h_ref[:, cols]
        o_ref[:, cols] = _gelu_tanh(y).astype(o_ref.dtype)


def s5_mixer(x, g, a_re, a_im, bb_re, bb_im, c_re_blk, c_im_blk, d_skip, bsz):
    t, d = x.shape
    seq = t // bsz
    steps = min(S5_STEPS, seq)
    rows = steps * bsz
    n_state = a_re.shape[1]
    wbytes = 2 * (bb_re.size + c_re_blk.size) * 2
    est = 2 * rows * d * 4 + 2 * rows * d * 2 + 2 * wbytes + 2 * rows * n_state * 4 + 4 * rows * 1024 * 4
    const3 = lambda i: (0, 0, 0)
    const2 = lambda i: (0, 0)
    return pl.pallas_call(
        functools.partial(_s5_kernel, bsz=bsz, steps=steps),
        grid=(seq // steps,),
        in_specs=[
            pl.BlockSpec((rows, d), lambda i: (i, 0)),
            pl.BlockSpec((1, d), const2),
            pl.BlockSpec((1, n_state), const2),
            pl.BlockSpec((1, n_state), const2),
            pl.BlockSpec(bb_re.shape, const3),
            pl.BlockSpec(bb_im.shape, const3),
            pl.BlockSpec(c_re_blk.shape, const3),
            pl.BlockSpec(c_im_blk.shape, const3),
            pl.BlockSpec((1, d), const2),
        ],
        out_specs=pl.BlockSpec((rows, d), lambda i: (i, 0)),
        out_shape=jax.ShapeDtypeStruct((t, d), BF16),
        scratch_shapes=[
            pltpu.VMEM((rows, n_state), F32),
            pltpu.VMEM((rows, n_state), F32),
            pltpu.VMEM((bsz, n_state), F32),
            pltpu.VMEM((bsz, n_state), F32),
            pltpu.VMEM((rows, d), F32),
        ],
        compiler_params=_params(("arbitrary",), est),
        name="s5_mixer",
    )(x, g, a_re, a_im, bb_re, bb_im, c_re_blk, c_im_blk, d_skip)


def _mla_proj_kernel(x_ref, g_ref, win_ref, qn_ref, kvn_ref, wa_ref, wb_ref, wk_ref, wv_ref, cos_ref, sin_ref,
                     q_ref, k_ref, v_ref, *, scale):
    h = _rms(x_ref[...], g_ref[...]).astype(BF16)
    c = jnp.dot(h, win_ref[...], preferred_element_type=F32)
    qc = _rms(c[:, :MLA_Q_RANK], qn_ref[...]).astype(BF16)
    kvc = _rms(c[:, MLA_Q_RANK:MLA_Q_RANK + MLA_KV_RANK], kvn_ref[...]).astype(BF16)
    cos = cos_ref[...]
    sin = sin_ref[...]
    base = MLA_Q_RANK + MLA_KV_RANK
    k_rope = c[:, base:base + LANES] * cos + c[:, base + LANES:base + 2 * LANES] * sin
    lane = lax.broadcasted_iota(jnp.int32, (1, LANES), 1)
    q_cos = (cos + jnp.where(lane >= 2 * MLA_ROPE, 1.0, 0.0)) * scale
    q_sin = sin * scale
    k_in = jnp.concatenate([kvc, k_rope.astype(BF16)], axis=-1)
    per = 4
    for cblk in range(MLA_HEADS // per):
        cols = slice(cblk * per * LANES, (cblk + 1) * per * LANES)
        qa = jnp.dot(qc, wa_ref[:, cols], preferred_element_type=F32)
        qb = jnp.dot(qc, wb_ref[:, cols], preferred_element_type=F32)
        kk = jnp.dot(k_in, wk_ref[:, cols], preferred_element_type=F32)
        for jj in range(per):
            sl = slice(jj * LANES, (jj + 1) * LANES)
            q_ref[cblk * per + jj] = (qa[:, sl] * q_cos + qb[:, sl] * q_sin).astype(BF16)
            k_ref[cblk * per + jj] = kk[:, sl].astype(BF16)
    vv = jnp.dot(kvc, wv_ref[...], preferred_element_type=F32)
    for jj in range(MLA_HEADS // 2):
        v_ref[jj] = vv[:, jj * LANES:(jj + 1) * LANES].astype(BF16)


def mla_proj(x, bsz, seq, g, w_in, q_norm, kv_norm, wa, wb, wk, wv, cos_tab, sin_tab):
    d = D_MODEL
    tl = min(ROW_TILE, seq)
    scale = float((MLA_NOPE + MLA_ROPE) ** -0.5)
    wbytes = (w_in.size + wa.size + wb.size + wk.size + wv.size) * 2
    est = 2 * tl * d * 4 + 2 * wbytes + 2 * 5 * tl * 1024 * 2 + 8 * tl * 1024 * 4
    c2 = lambda i, b: (0, 0)
    nh = MLA_HEADS
    return pl.pallas_call(
        functools.partial(_mla_proj_kernel, scale=scale),
        grid=(seq // tl, bsz),
        in_specs=[
            _x_spec(False, tl, d),
            pl.BlockSpec((1, d), c2),
            pl.BlockSpec(w_in.shape, c2),
            pl.BlockSpec((1, MLA_Q_RANK), c2),
            pl.BlockSpec((1, MLA_KV_RANK), c2),
            pl.BlockSpec(wa.shape, c2),
            pl.BlockSpec(wb.shape, c2),
            pl.BlockSpec(wk.shape, c2),
            pl.BlockSpec(wv.shape, c2),
            pl.BlockSpec((tl, LANES), lambda i, b: (i, 0)),
            pl.BlockSpec((tl, LANES), lambda i, b: (i, 0)),
        ],
        out_specs=(
            pl.BlockSpec((None, nh, tl, LANES), lambda i, b: (b, 0, i, 0)),
            pl.BlockSpec((None, nh, tl, LANES), lambda i, b: (b, 0, i, 0)),
            pl.BlockSpec((None, nh // 2, tl, LANES), lambda i, b: (b, 0, i, 0)),
        ),
        out_shape=(
            jax.ShapeDtypeStruct((bsz, nh, seq, LANES), BF16),
            jax.ShapeDtypeStruct((bsz, nh, seq, LANES), BF16),
            jax.ShapeDtypeStruct((bsz, nh // 2, seq, LANES), BF16),
        ),
        compiler_params=_params(("parallel", "parallel"), est),
        name="mla_proj",
    )(x, g, w_in, q_norm, kv_norm, wa, wb, wk, wv, cos_tab, sin_tab)


def _mla_attn_kernel(q_ref, k_ref, v_ref, o_ref, *, blk):
    qi = pl.program_id(2)
    lane = lax.broadcasted_iota(jnp.int32, (1, LANES), 1)
    row = lax.broadcasted_iota(jnp.int32, (blk, blk), 0)
    col = lax.broadcasted_iota(jnp.int32, (blk, blk), 1)
    causal = col <= row

    def block(hh, qh, j, carry, mask):
        acc, m, l = carry
        start = pl.multiple_of(j * blk, blk)
        kb = k_ref[hh, pl.ds(start, blk), :]
        vb = v_ref[pl.ds(start, blk), :]
        s = lax.dot_general(qh, kb, (((1,), (1,)), ((), ())), preferred_element_type=F32)
        if mask is not None:
            s = jnp.where(mask, s, -jnp.inf)
        m_new = jnp.maximum(m, jnp.max(s, axis=-1, keepdims=True))
        alpha = jnp.exp(m - m_new)
        p = jnp.exp(s - m_new)
        l = alpha * l + jnp.sum(p, axis=-1, keepdims=True)
        acc = alpha * acc + jnp.dot(p.astype(BF16), vb, preferred_element_type=F32)
        return acc, m_new, l

    outs = []
    for hh in range(2):
        qh = q_ref[hh]
        carry = (jnp.zeros((blk, LANES), F32), jnp.full((blk, 1), -jnp.inf, F32), jnp.zeros((blk, 1), F32))
        carry = block(hh, qh, qi, carry, causal)
        carry = lax.fori_loop(0, qi, lambda s, c: block(hh, qh, qi - 1 - s, c, None), carry)
        acc, _, l = carry
        outs.append(acc / l)
    o_ref[...] = jnp.where(lane < MLA_V, outs[0], outs[1]).astype(o_ref.dtype)


def mla_attention(q, k, v, bsz, seq):
    blk = min(ATT_BLOCK, seq)
    n_pairs = MLA_HEADS // 2
    est = 2 * (2 * seq * LANES * 2 + seq * LANES * 2) + 8 * blk * blk * 4 + 8 * blk * LANES * 4
    return pl.pallas_call(
        functools.partial(_mla_attn_kernel, blk=blk),
        grid=(bsz, n_pairs, seq // blk),
        in_specs=[
            pl.BlockSpec((None, 2, blk, LANES), lambda b, h, i: (b, h, i, 0)),
            pl.BlockSpec((None, 2, seq, LANES), lambda b, h, i: (b, h, 0, 0)),
            pl.BlockSpec((None, None, seq, LANES), lambda b, h, i: (b, h, 0, 0)),
        ],
        out_specs=pl.BlockSpec((blk, LANES), lambda b, h, i: (i, b * n_pairs + h)),
        out_shape=jax.ShapeDtypeStruct((seq, bsz * D_MODEL), BF16),
        compiler_params=_params(("parallel", "parallel", "arbitrary"), est),
        name="mla_attention",
    )(q, k, v)


def _final_norm_kernel(x_ref, g_ref, o_ref):
    o_ref[...] = _rms(x_ref[...], g_ref[...])


def final_norm(x, g, bsz, seq):
    d = D_MODEL
    tl = min(ROW_TILE, seq)
    return pl.pallas_call(
        _final_norm_kernel,
        grid=(seq // tl, bsz),
        in_specs=[_x_spec(False, tl, d), pl.BlockSpec((1, d), lambda i, b: (0, 0))],
        out_specs=pl.BlockSpec((None, tl, d), lambda i, b: (b, i, 0)),
        out_shape=jax.ShapeDtypeStruct((bsz, seq, d), F32),
        compiler_params=_params(("parallel", "parallel"), 8 * tl * d * 4),
        name="final_norm",
    )(x, g)


def _row(v):
    return v.reshape(1, -1).astype(F32)


def _sb_qkv_weight(w_qkv):
    scale = SB_HEAD_DIM ** -0.5
    col_scale = jnp.concatenate([jnp.full((D_MODEL,), scale, F32), jnp.ones((2 * D_MODEL,), F32)])
    return (w_qkv * col_scale[None, :]).astype(BF16)


def _s5_block_diag(b_re, b_im, c_re, c_im):
    per = S5_SLAB // S5_GROUP
    n_slabs = S5_GROUPS // per
    eye = jnp.eye(per, dtype=F32)

    def b_blk(b):
        b4 = b.reshape(n_slabs, per, S5_STATE, S5_GROUP)
        return jnp.einsum('sgpc,gh->sgchp', b4, eye).reshape(n_slabs, per * S5_GROUP, per * S5_STATE)

    def c_blk(c):
        c4 = c.reshape(n_slabs, per, S5_GROUP, S5_STATE)
        return jnp.einsum('sgcp,gh->sgphc', c4, eye).reshape(n_slabs, per * S5_STATE, per * S5_GROUP)

    return b_blk(b_re), b_blk(b_im), c_blk(c_re).astype(BF16), c_blk(c_im).astype(BF16)


def _rotate_half_cols(w):
    half = w.shape[-1] // 2
    return jnp.concatenate([-w[..., half:], w[..., :half]], axis=-1)


def _mla_weights(w_in, w_q_b, w_kv_b):
    base = MLA_Q_RANK + MLA_KV_RANK
    w_kr = w_in[:, base:]
    pad = jnp.zeros((D_MODEL, LANES - MLA_ROPE), F32)
    w_in2 = jnp.concatenate([w_in[:, :base], w_kr, pad, _rotate_half_cols(w_kr), pad], axis=-1).astype(BF16)
    wq = w_q_b.reshape(MLA_Q_RANK, MLA_HEADS, MLA_NOPE + MLA_ROPE)
    q_nope, q_rope = wq[..., :MLA_NOPE], wq[..., MLA_NOPE:]
    zq = jnp.zeros((MLA_Q_RANK, MLA_HEADS, MLA_ROPE), F32)
    wa = jnp.concatenate([q_rope, zq, q_nope], axis=-1).reshape(MLA_Q_RANK, MLA_HEADS * LANES).astype(BF16)
    wb = jnp.concatenate([_rotate_half_cols(q_rope), zq, jnp.zeros_like(q_nope)], axis=-1)
    wb = wb.reshape(MLA_Q_RANK, MLA_HEADS * LANES).astype(BF16)
    wkv = w_kv_b.reshape(MLA_KV_RANK, MLA_HEADS, MLA_NOPE + MLA_V)
    k_nope, v_w = wkv[..., :MLA_NOPE], wkv[..., MLA_NOPE:]
    zk = jnp.zeros((MLA_KV_RANK, MLA_HEADS, 2 * MLA_ROPE), F32)
    wk_top = jnp.concatenate([zk, k_nope], axis=-1).reshape(MLA_KV_RANK, MLA_HEADS * LANES)
    route = jnp.concatenate([jnp.eye(LANES, MLA_ROPE, dtype=F32), jnp.zeros((LANES, LANES - MLA_ROPE), F32)], axis=-1)
    wk_bot = jnp.tile(route, (1, MLA_HEADS))
    wk = jnp.concatenate([wk_top, wk_bot], axis=0).astype(BF16)
    wv = v_w.reshape(MLA_KV_RANK, MLA_HEADS * MLA_V).astype(BF16)
    return w_in2, wa, wb, wk, wv


def _rope_tables(seq, bsz):
    pos = jnp.arange(seq, dtype=F32)
    inv_freq = ROPE_THETA ** (-jnp.arange(0, MLA_ROPE, 2, dtype=F32) / MLA_ROPE)
    ang = pos[:, None] * inv_freq[None, :]
    zeros = jnp.zeros((seq, LANES - MLA_ROPE), F32)
    cos = jnp.concatenate([jnp.cos(ang), jnp.cos(ang), zeros], axis=-1)
    sin = jnp.concatenate([jnp.sin(ang), jnp.sin(ang), zeros], axis=-1)
    return cos, sin


def moe_layer(flat, g_ffn, w_router, b_router, w_gate_up, w_down):
    w_r = jnp.pad(w_router, ((0, 0), (0, LANES - N_EXPERTS)))
    b_r = jnp.pad(b_router, (0, LANES - N_EXPERTS)).reshape(1, LANES)
    gates = router(flat, g_ffn, w_r, b_r)
    return ffn(flat, g_ffn, w_gate_up.astype(BF16), w_down.astype(BF16), gates)


def kernel(x, norm_mix, norm_ffn, final_norm_g, sb_w_qkv, sb_w_o, s5_lambda_re, s5_lambda_im, s5_log_dt, s5_b_re,
           s5_b_im, s5_c_re, s5_c_im, s5_d, s5_w_glu, mla_w_in, mla_q_norm, mla_w_q_b, mla_kv_norm, mla_w_kv_b,
           mla_w_o, ffn_w_gate_up, ffn_w_down, moe_w_router, moe_b_router, moe_w_gate_up, moe_w_down):
    bsz, seq, d = x.shape
    t = bsz * seq
    blk = min(ATT_BLOCK, seq)
    tri = (jnp.arange(blk)[:, None] > jnp.arange(blk)[None, :]).astype(BF16)
    cos_tab, sin_tab = _rope_tables(seq, bsz)

    cur = x
    is_bld = True
    for i in range(DEPTH):
        kind, slot = i % N_MIXERS, i // N_MIXERS
        g_mix = _row(norm_mix[i])
        if kind == 0:
            qkv = norm_proj_blocks(cur, is_bld, bsz, seq, g_mix, _sb_qkv_weight(sb_w_qkv[slot]))
            o = sb_attention(qkv, bsz, seq, tri)
            cur = proj_residual(o, cur, is_bld, bsz, seq, sb_w_o[slot].astype(BF16))
        elif kind == 1:
            b_re_blk, b_im_blk, c_re_blk, c_im_blk = _s5_block_diag(
                s5_b_re[slot], s5_b_im[slot], s5_c_re[slot], s5_c_im[slot])
            log_dt_row = jnp.repeat(s5_log_dt[slot], S5_STATE).reshape(1, -1)
            a_re, a_im, bb_re, bb_im = s5_discretise(
                _row(s5_lambda_re[slot]), _row(s5_lambda_im[slot]), log_dt_row, b_re_blk, b_im_blk)
            y = s5_mixer(cur.reshape(t, d), g_mix, a_re, a_im, bb_re, bb_im, c_re_blk, c_im_blk,
                         _row(s5_d[slot]), bsz)
            cur = proj_residual(y.reshape(seq, bsz * d), cur, False, bsz, seq, s5_w_glu[slot].astype(BF16), glu=True)
        else:
            w_in2, wa, wb, wk, wv = _mla_weights(mla_w_in[slot], mla_w_q_b[slot], mla_w_kv_b[slot])
            q, k, v = mla_proj(cur, bsz, seq, g_mix, w_in2, _row(mla_q_norm[slot]), _row(mla_kv_norm[slot]),
                               wa, wb, wk, wv, cos_tab, sin_tab)
            o = mla_attention(q, k, v, bsz, seq)
            cur = proj_residual(o, cur, False, bsz, seq, mla_w_o[slot].astype(BF16))
        is_bld = False
        flat = cur.reshape(t, d)
        g_ffn = _row(norm_ffn[i])
        if i % 2 == 0:
            flat = ffn(flat, g_ffn, ffn_w_gate_up[i // 2][None].astype(BF16), ffn_w_down[i // 2][None].astype(BF16))
        else:
            flat = moe_layer(flat, g_ffn, moe_w_router[i // 2], moe_b_router[i // 2], moe_w_gate_up[i // 2],
                             moe_w_down[i // 2])
        cur = flat.reshape(seq, bsz * d)
    return final_norm(cur, _row(final_norm_g), bsz, seq)
```

```python
import functools
import math

import jax
import jax.numpy as jnp
from jax import lax
from jax.experimental import pallas as pl
from jax.experimental.pallas import tpu as pltpu

F32 = jnp.float32
BF16 = jnp.bfloat16

D_MODEL = 1024
DEPTH = 4
N_MIXERS = 3
RMS_EPS = 1e-6
SB_HEADS = 16
SB_HEAD_DIM = D_MODEL // SB_HEADS
S5_GROUP = 16
S5_GROUPS = D_MODEL // S5_GROUP
S5_STATE = 64
MLA_HEADS = 16
MLA_NOPE = 64
MLA_ROPE = 32
MLA_V = 64
MLA_Q_RANK = 384
MLA_KV_RANK = 256
ROPE_THETA = 10000.0
D_FF = 2816
N_EXPERTS = 8
TOP_K = 2
D_FF_EXPERT = 1408

LANES = 128
SUBLANES = 8
MXU_DIM = 256
VMEM_BYTES_V7X = 64 * 1024 * 1024
VMEM_CAP = VMEM_BYTES_V7X - 8 * 1024 * 1024

ROW_TILE = 512
ATT_BLOCK = MXU_DIM
S5_STEPS = 64
S5_SLAB = MXU_DIM
S5_LANE_CHUNK = 512
FF_CHUNK = 1408


def _params(semantics, est_bytes):
    limit = int(min(VMEM_CAP, max(32 * 1024 * 1024, est_bytes * 5 // 4)))
    return pltpu.CompilerParams(dimension_semantics=semantics, vmem_limit_bytes=limit)


def _rms(xf, g):
    ms = jnp.mean(xf * xf, axis=-1, keepdims=True)
    return xf * lax.rsqrt(ms + RMS_EPS) * g


def _sigmoid(v):
    return 1.0 / (1.0 + jnp.exp(-v))


def _x_spec(x_is_bld, tl, d):
    if x_is_bld:
        return pl.BlockSpec((None, tl, d), lambda i, b: (b, i, 0))
    return pl.BlockSpec((tl, d), lambda i, b: (i, b))


def _norm_proj_blocks_kernel(x_ref, g_ref, w_ref, o_ref):
    h = _rms(x_ref[...], g_ref[...]).astype(BF16)
    nb = o_ref.shape[0]
    per = 4
    for c in range(nb // per):
        r = jnp.dot(h, w_ref[:, c * per * LANES:(c + 1) * per * LANES], preferred_element_type=F32)
        for jj in range(per):
            o_ref[c * per + jj] = r[:, jj * LANES:(jj + 1) * LANES].astype(BF16)


def norm_proj_blocks(x, x_is_bld, bsz, seq, g, w):
    d, n = w.shape
    nb = n // LANES
    tl = min(ROW_TILE, seq)
    est = 2 * tl * d * 4 + 2 * d * n * 2 + 2 * n * tl * 2 + 4 * tl * 512 * 4
    return pl.pallas_call(
        _norm_proj_blocks_kernel,
        grid=(seq // tl, bsz),
        in_specs=[
            _x_spec(x_is_bld, tl, d),
            pl.BlockSpec((1, d), lambda i, b: (0, 0)),
            pl.BlockSpec((d, n), lambda i, b: (0, 0)),
        ],
        out_specs=pl.BlockSpec((nb, None, tl, LANES), lambda i, b: (0, b, i, 0)),
        out_shape=jax.ShapeDtypeStruct((nb, bsz, seq, LANES), BF16),
        compiler_params=_params(("parallel", "parallel"), est),
        name="norm_proj_blocks",
    )(x, g, w)


def _sb_attn_kernel(q_ref, k_ref, v_ref, u_ref, o_ref, *, blk, n_blocks):
    lane = lax.broadcasted_iota(jnp.int32, (1, LANES), 1)
    row = lax.broadcasted_iota(jnp.int32, (blk, blk), 0)
    col = lax.broadcasted_iota(jnp.int32, (blk, blk), 1)
    strict = col < row
    u = u_ref[...]
    nt = (((1,), (1,)), ((), ()))

    def log_one_minus_beta(z):
        return -(jnp.maximum(z, 0.0) + jnp.log(1.0 + jnp.exp(-jnp.abs(z))))

    for qi in range(n_blocks):
        r0 = qi * blk
        q = q_ref[r0:r0 + blk, :]
        accs = []
        for hh in range(2):
            head_lanes = (lane >= SB_HEAD_DIM) if hh else (lane < SB_HEAD_DIM)
            qh = jnp.where(head_lanes, q, jnp.zeros_like(q))
            z = lax.dot_general(qh, k_ref[r0:r0 + blk, :], nt, preferred_element_type=F32)
            lom = jnp.where(strict, log_one_minus_beta(z), 0.0)
            tail = jnp.dot(lom.astype(BF16), u, preferred_element_type=F32)
            w = jnp.where(strict, jnp.exp(lom + z + tail), 0.0)
            acc = jnp.dot(w.astype(BF16), v_ref[r0:r0 + blk, :], preferred_element_type=F32)
            run = jnp.sum(lom, axis=-1, keepdims=True)
            if qi > 0:
                z_off = lax.dot_general(qh, k_ref[0:r0, :], nt, preferred_element_type=F32)
                for j in range(qi - 1, -1, -1):
                    z = z_off[:, j * blk:(j + 1) * blk]
                    lom = log_one_minus_beta(z)
                    tail = jnp.dot(lom.astype(BF16), u, preferred_element_type=F32) + run
                    w = jnp.exp(lom + z + tail)
                    acc = acc + jnp.dot(w.astype(BF16), v_ref[j * blk:(j + 1) * blk, :], preferred_element_type=F32)
                    if j > 0:
                        run = run + jnp.sum(lom, axis=-1, keepdims=True)
            accs.append(acc)
        o_ref[r0:r0 + blk, :] = jnp.where(lane < SB_HEAD_DIM, accs[0], accs[1]).astype(o_ref.dtype)


def sb_attention(qkv, bsz, seq, tri):
    blk = min(ATT_BLOCK, seq)
    n_pairs = SB_HEADS // 2
    est = 8 * seq * LANES * 2 + 6 * blk * seq * 4
    return pl.pallas_call(
        functools.partial(_sb_attn_kernel, blk=blk, n_blocks=seq // blk),
        grid=(bsz, n_pairs),
        in_specs=[
            pl.BlockSpec((None, None, seq, LANES), lambda b, h: (h, b, 0, 0)),
            pl.BlockSpec((None, None, seq, LANES), lambda b, h: (n_pairs + h, b, 0, 0)),
            pl.BlockSpec((None, None, seq, LANES), lambda b, h: (2 * n_pairs + h, b, 0, 0)),
            pl.BlockSpec((blk, blk), lambda b, h: (0, 0)),
        ],
        out_specs=pl.BlockSpec((seq, LANES), lambda b, h: (0, b * n_pairs + h)),
        out_shape=jax.ShapeDtypeStruct((seq, bsz * D_MODEL), BF16),
        compiler_params=_params(("parallel", "parallel"), est),
        name="sb_attention",
    )(qkv, qkv, qkv, tri)


def _proj_residual_kernel(a_ref, x_ref, w_ref, o_ref, *, glu):
    r = jnp.dot(a_ref[...], w_ref[...], preferred_element_type=F32)
    if glu:
        n = o_ref.shape[-1]
        r = r[:, :n] * _sigmoid(r[:, n:])
    o_ref[...] = x_ref[...] + r


def proj_residual(a, x, x_is_bld, rows, ncol, w, glu=False):
    k, n_w = w.shape
    n = n_w // 2 if glu else n_w
    tl = min(ROW_TILE, rows)
    est = 2 * tl * k * 2 + 4 * tl * n * 4 + 2 * k * n_w * 2 + 2 * tl * n_w * 4
    return pl.pallas_call(
        functools.partial(_proj_residual_kernel, glu=glu),
        grid=(rows // tl, ncol),
        in_specs=[
            pl.BlockSpec((tl, k), lambda i, b: (i, b)),
            _x_spec(x_is_bld, tl, n),
            pl.BlockSpec((k, n_w), lambda i, b: (0, 0)),
        ],
        out_specs=pl.BlockSpec((tl, n), lambda i, b: (i, b)),
        out_shape=jax.ShapeDtypeStruct((rows, ncol * n), F32),
        compiler_params=_params(("parallel", "parallel"), est),
        name="proj_residual_glu" if glu else "proj_residual",
    )(a, x, w)


def _ffn_kernel(x_ref, g_ref, gates_ref, wg_ref, wu_ref, wd_ref, o_ref, h_ref, *, chunks_per_expert, gated):
    j = pl.program_id(2)

    @pl.when(j == 0)
    def _():
        xf = x_ref[...]
        h_ref[...] = _rms(xf, g_ref[...]).astype(BF16)
        o_ref[...] = xf

    h = h_ref[...]
    gv = jnp.dot(h, wg_ref[...], preferred_element_type=F32)
    uv = jnp.dot(h, wu_ref[...], preferred_element_type=F32)
    act = gv * _sigmoid(gv) * uv
    if gated:
        e = j // chunks_per_expert
        lane = lax.broadcasted_iota(jnp.int32, (1, LANES), 1)
        ge = jnp.sum(jnp.where(lane == e, gates_ref[...], 0.0), axis=-1, keepdims=True)
        act = act * ge
    o_ref[...] += jnp.dot(act.astype(BF16), wd_ref[...], preferred_element_type=F32)


def ffn(x, g, w_gate_up, w_down, gates=None):
    n_e, f, d = w_down.shape
    rows, ncol = x.shape[0], x.shape[1] // d
    tf = FF_CHUNK
    cpe = f // tf
    tm = min(ROW_TILE, rows)
    gated = gates is not None
    if not gated:
        gates = jnp.zeros((SUBLANES, LANES), F32)
    gates_spec = (pl.BlockSpec((tm, LANES), lambda i, c, j: (i, c)) if gated
                  else pl.BlockSpec((SUBLANES, LANES), lambda i, c, j: (0, 0)))
    est = 4 * tm * d * 4 + tm * d * 2 + 6 * d * tf * 2 + 3 * tm * tf * 4
    return pl.pallas_call(
        functools.partial(_ffn_kernel, chunks_per_expert=cpe, gated=gated),
        grid=(rows // tm, ncol, n_e * cpe),
        in_specs=[
            pl.BlockSpec((tm, d), lambda i, c, j: (i, c)),
            pl.BlockSpec((1, d), lambda i, c, j: (0, 0)),
            gates_spec,
            pl.BlockSpec((None, d, tf), lambda i, c, j: (j // cpe, 0, j % cpe)),
            pl.BlockSpec((None, d, tf), lambda i, c, j: (j // cpe, 0, cpe + j % cpe)),
            pl.BlockSpec((None, tf, d), lambda i, c, j: (j // cpe, j % cpe, 0)),
        ],
        out_specs=pl.BlockSpec((tm, d), lambda i, c, j: (i, c)),
        out_shape=jax.ShapeDtypeStruct(x.shape, F32),
        scratch_shapes=[pltpu.VMEM((tm, d), BF16)],
        compiler_params=_params(("parallel", "parallel", "arbitrary"), est),
        name="ffn_gated" if gated else "ffn",
    )(x, g, gates, w_gate_up, w_gate_up, w_down)


def _router_kernel(x_ref, g_ref, w_ref, b_ref, o_ref):
    h = _rms(x_ref[...], g_ref[...])
    logits = jnp.dot(h, w_ref[...], preferred_element_type=F32, precision=lax.Precision.HIGHEST) + b_ref[...]
    lane = lax.broadcasted_iota(jnp.int32, logits.shape, 1).astype(F32)
    neg = jnp.float32(-jnp.inf)
    logits = jnp.where(lane < N_EXPERTS, logits, neg)
    m1 = jnp.max(logits, axis=-1, keepdims=True)
    i1 = jnp.min(jnp.where(logits == m1, lane, float(LANES)), axis=-1, keepdims=True)
    rest = jnp.where(lane == i1, neg, logits)
    m2 = jnp.max(rest, axis=-1, keepdims=True)
    i2 = jnp.min(jnp.where(rest == m2, lane, float(LANES)), axis=-1, keepdims=True)
    e2 = jnp.exp(m2 - m1)
    w1 = 1.0 / (1.0 + e2)
    w2 = e2 * w1
    o_ref[...] = jnp.where(lane == i1, w1, jnp.where(lane == i2, w2, 0.0))


def router(x, g, w_pad, b_pad):
    d = w_pad.shape[0]
    rows, ncol = x.shape[0], x.shape[1] // d
    tm = min(ROW_TILE, rows)
    est = 2 * tm * d * 4 + 2 * d * LANES * 4 + 4 * tm * LANES * 4 + 2 * tm * d * 4
    return pl.pallas_call(
        _router_kernel,
        grid=(rows // tm, ncol),
        in_specs=[
            pl.BlockSpec((tm, d), lambda i, c: (i, c)),
            pl.BlockSpec((1, d), lambda i, c: (0, 0)),
            pl.BlockSpec((d, LANES), lambda i, c: (0, 0)),
            pl.BlockSpec((1, LANES), lambda i, c: (0, 0)),
        ],
        out_specs=pl.BlockSpec((tm, LANES), lambda i, c: (i, c)),
        out_shape=jax.ShapeDtypeStruct((rows, ncol * LANES), F32),
        compiler_params=_params(("parallel", "parallel"), est),
        name="router",
    )(x, g, w_pad, b_pad)


def _s5_discretise_kernel(lre_ref, lim_ref, ldt_ref, bre_ref, bim_ref, are_ref, aim_ref, bbre_ref, bbim_ref):
    lam_re = lre_ref[...]
    lam_im = lim_ref[...]
    dt = jnp.exp(ldt_ref[...])
    mag = jnp.exp(lam_re * dt)
    ang = lam_im * dt
    a_re = mag * jnp.cos(ang)
    a_im = mag * jnp.sin(ang)
    den = lam_re * lam_re + lam_im * lam_im
    num_re = a_re - 1.0
    f_re = (num_re * lam_re + a_im * lam_im) / den
    f_im = (a_im * lam_re - num_re * lam_im) / den
    are_ref[...] = a_re
    aim_ref[...] = a_im
    n_slabs, _, width = bre_ref.shape
    for s in range(n_slabs):
        fr = f_re[:, s * width:(s + 1) * width]
        fi = f_im[:, s * width:(s + 1) * width]
        br = bre_ref[s]
        bi = bim_ref[s]
        bbre_ref[s] = (fr * br - fi * bi).astype(BF16)
        bbim_ref[s] = (fr * bi + fi * br).astype(BF16)


def s5_discretise(lam_re_row, lam_im_row, log_dt_row, b_re_blk, b_im_blk):
    n_state = lam_re_row.shape[1]
    shp = b_re_blk.shape
    return pl.pallas_call(
        _s5_discretise_kernel,
        out_shape=(
            jax.ShapeDtypeStruct((1, n_state), F32),
            jax.ShapeDtypeStruct((1, n_state), F32),
            jax.ShapeDtypeStruct(shp, BF16),
            jax.ShapeDtypeStruct(shp, BF16),
        ),
        name="s5_discretise",
    )(lam_re_row, lam_im_row, log_dt_row, b_re_blk, b_im_blk)


def _gelu_tanh(y):
    c = math.sqrt(2.0 / math.pi)
    return 0.5 * y * (1.0 + jnp.tanh(c * (y + 0.044715 * (y * y * y))))


def _s5_kernel(x_ref, g_ref, are_ref, aim_ref, bbre_ref, bbim_ref, cre_ref, cim_ref, d_ref, o_ref,
               sre_ref, sim_ref, stre_ref, stim_ref, h_ref, *, bsz, steps):
    i = pl.program_id(0)

    @pl.when(i == 0)
    def _():
        stre_ref[...] = jnp.zeros_like(stre_ref)
        stim_ref[...] = jnp.zeros_like(stim_ref)

    h_ref[...] = _rms(x_ref[...], g_ref[...])
    n_slabs, slab, width = bbre_ref.shape
    for s in range(n_slabs):
        hs = h_ref[:, s * slab:(s + 1) * slab].astype(BF16)
        sre_ref[:, s * width:(s + 1) * width] = jnp.dot(hs, bbre_ref[s], preferred_element_type=F32)
        sim_ref[:, s * width:(s + 1) * width] = jnp.dot(hs, bbim_ref[s], preferred_element_type=F32)

    n_state = sre_ref.shape[1]
    cw = S5_LANE_CHUNK
    for c in range(n_state // cw):
        lanes = slice(c * cw, (c + 1) * cw)
        ar = jnp.broadcast_to(are_ref[:, lanes], (bsz, cw))
        ai = jnp.broadcast_to(aim_ref[:, lanes], (bsz, cw))

        def step(t, carry):
            xr, xi = carry
            r0 = pl.multiple_of(t * bsz, bsz)
            nr = ar * xr - ai * xi + sre_ref[pl.ds(r0, bsz), lanes]
            ni = ar * xi + ai * xr + sim_ref[pl.ds(r0, bsz), lanes]
            sre_ref[pl.ds(r0, bsz), lanes] = nr
            sim_ref[pl.ds(r0, bsz), lanes] = ni
            return nr, ni

        xr, xi = lax.fori_loop(0, steps, step, (stre_ref[:, lanes], stim_ref[:, lanes]), unroll=8)
        stre_ref[:, lanes] = xr
        stim_ref[:, lanes] = xi

    dsk = d_ref[...]
    for s in range(n_slabs):
        xr = sre_ref[:, s * width:(s + 1) * width].astype(BF16)
        xi = sim_ref[:, s * width:(s + 1) * width].astype(BF16)
        y = (jnp.dot(xr, cre_ref[s], preferred_element_type=F32)
             - jnp.dot(xi, cim_ref[s], preferred_element_type=F32))
        cols = slice(s * slab, (s + 1) * slab)
        y = y + dsk[:, cols] * h_ref[:, cols]
        o_ref[:, cols] = _gelu_tanh(y).astype(o_ref.dtype)


def s5_mixer(x, g, a_re, a_im, bb_re, bb_im, c_re_blk, c_im_blk, d_skip, bsz):
    t, d = x.shape
    seq = t // bsz
    steps = min(S5_STEPS, seq)
    rows = steps * bsz
    n_state = a_re.shape[1]
    wbytes = 2 * (bb_re.size + c_re_blk.size) * 2
    est = 2 * rows * d * 4 + 2 * rows * d * 2 + 2 * wbytes + 2 * rows * n_state * 4 + 4 * rows * 1024 * 4
    const3 = lambda i: (0, 0, 0)
    const2 = lambda i: (0, 0)
    return pl.pallas_call(
        functools.partial(_s5_kernel, bsz=bsz, steps=steps),
        grid=(seq // steps,),
        in_specs=[
            pl.BlockSpec((rows, d), lambda i: (i, 0)),
            pl.BlockSpec((1, d), const2),
            pl.BlockSpec((1, n_state), const2),
            pl.BlockSpec((1, n_state), const2),
            pl.BlockSpec(bb_re.shape, const3),
            pl.BlockSpec(bb_im.shape, const3),
            pl.BlockSpec(c_re_blk.shape, const3),
            pl.BlockSpec(c_im_blk.shape, const3),
            pl.BlockSpec((1, d), const2),
        ],
        out_specs=pl.BlockSpec((rows, d), lambda i: (i, 0)),
        out_shape=jax.ShapeDtypeStruct((t, d), BF16),
        scratch_shapes=[
            pltpu.VMEM((rows, n_state), F32),
            pltpu.VMEM((rows, n_state), F32),
            pltpu.VMEM((bsz, n_state), F32),
            pltpu.VMEM((bsz, n_state), F32),
            pltpu.VMEM((rows, d), F32),
        ],
        compiler_params=_params(("arbitrary",), est),
        name="s5_mixer",
    )(x, g, a_re, a_im, bb_re, bb_im, c_re_blk, c_im_blk, d_skip)


def _mla_proj_kernel(x_ref, g_ref, win_ref, qn_ref, kvn_ref, wa_ref, wb_ref, wk_ref, wv_ref, cos_ref, sin_ref,
                     q_ref, k_ref, v_ref, *, scale):
    h = _rms(x_ref[...], g_ref[...]).astype(BF16)
    c = jnp.dot(h, win_ref[...], preferred_element_type=F32)
    qc = _rms(c[:, :MLA_Q_RANK], qn_ref[...]).astype(BF16)
    kvc = _rms(c[:, MLA_Q_RANK:MLA_Q_RANK + MLA_KV_RANK], kvn_ref[...]).astype(BF16)
    cos = cos_ref[...]
    sin = sin_ref[...]
    base = MLA_Q_RANK + MLA_KV_RANK
    k_rope = c[:, base:base + LANES] * cos + c[:, base + LANES:base + 2 * LANES] * sin
    lane = lax.broadcasted_iota(jnp.int32, (1, LANES), 1)
    q_cos = (cos + jnp.where(lane >= 2 * MLA_ROPE, 1.0, 0.0)) * scale
    q_sin = sin * scale
    k_in = jnp.concatenate([kvc, k_rope.astype(BF16)], axis=-1)
    per = 4
    for cblk in range(MLA_HEADS // per):
        cols = slice(cblk * per * LANES, (cblk + 1) * per * LANES)
        qa = jnp.dot(qc, wa_ref[:, cols], preferred_element_type=F32)
        qb = jnp.dot(qc, wb_ref[:, cols], preferred_element_type=F32)
        kk = jnp.dot(k_in, wk_ref[:, cols], preferred_element_type=F32)
        for jj in range(per):
            sl = slice(jj * LANES, (jj + 1) * LANES)
            q_ref[cblk * per + jj] = (qa[:, sl] * q_cos + qb[:, sl] * q_sin).astype(BF16)
            k_ref[cblk * per + jj] = kk[:, sl].astype(BF16)
    vv = jnp.dot(kvc, wv_ref[...], preferred_element_type=F32)
    for jj in range(MLA_HEADS // 2):
        v_ref[jj] = vv[:, jj * LANES:(jj + 1) * LANES].astype(BF16)


def mla_proj(x, bsz, seq, g, w_in, q_norm, kv_norm, wa, wb, wk, wv, cos_tab, sin_tab):
    d = D_MODEL
    tl = min(ROW_TILE, seq)
    scale = float((MLA_NOPE + MLA_ROPE) ** -0.5)
    wbytes = (w_in.size + wa.size + wb.size + wk.size + wv.size) * 2
    est = 2 * tl * d * 4 + 2 * wbytes + 2 * 5 * tl * 1024 * 2 + 8 * tl * 1024 * 4
    c2 = lambda i, b: (0, 0)
    nh = MLA_HEADS
    return pl.pallas_call(
        functools.partial(_mla_proj_kernel, scale=scale),
        grid=(seq // tl, bsz),
        in_specs=[
            _x_spec(False, tl, d),
            pl.BlockSpec((1, d), c2),
            pl.BlockSpec(w_in.shape, c2),
            pl.BlockSpec((1, MLA_Q_RANK), c2),
            pl.BlockSpec((1, MLA_KV_RANK), c2),
            pl.BlockSpec(wa.shape, c2),
            pl.BlockSpec(wb.shape, c2),
            pl.BlockSpec(wk.shape, c2),
            pl.BlockSpec(wv.shape, c2),
            pl.BlockSpec((tl, LANES), lambda i, b: (i, 0)),
            pl.BlockSpec((tl, LANES), lambda i, b: (i, 0)),
        ],
        out_specs=(
            pl.BlockSpec((None, nh, tl, LANES), lambda i, b: (b, 0, i, 0)),
            pl.BlockSpec((None, nh, tl, LANES), lambda i, b: (b, 0, i, 0)),
            pl.BlockSpec((None, nh // 2, tl, LANES), lambda i, b: (b, 0, i, 0)),
        ),
        out_shape=(
            jax.ShapeDtypeStruct((bsz, nh, seq, LANES), BF16),
            jax.ShapeDtypeStruct((bsz, nh, seq, LANES), BF16),
            jax.ShapeDtypeStruct((bsz, nh // 2, seq, LANES), BF16),
        ),
        compiler_params=_params(("parallel", "parallel"), est),
        name="mla_proj",
    )(x, g, w_in, q_norm, kv_norm, wa, wb, wk, wv, cos_tab, sin_tab)


def _mla_attn_kernel(q_ref, k_ref, v_ref, o_ref, *, blk, n_blocks):
    lane = lax.broadcasted_iota(jnp.int32, (1, LANES), 1)
    row = lax.broadcasted_iota(jnp.int32, (blk, blk), 0)
    col = lax.broadcasted_iota(jnp.int32, (blk, blk), 1)
    causal = col <= row
    nt = (((1,), (1,)), ((), ()))
    for qi in range(n_blocks):
        r0 = qi * blk
        outs = []
        for hh in range(2):
            qh = q_ref[hh, r0:r0 + blk, :]
            s_d = lax.dot_general(qh, k_ref[hh, r0:r0 + blk, :], nt, preferred_element_type=F32)
            s_d = jnp.where(causal, s_d, -jnp.inf)
            m = jnp.max(s_d, axis=-1, keepdims=True)
            if qi > 0:
                s_o = lax.dot_general(qh, k_ref[hh, 0:r0, :], nt, preferred_element_type=F32)
                m = jnp.maximum(m, jnp.max(s_o, axis=-1, keepdims=True))
                p_o = jnp.exp(s_o - m)
                l = jnp.sum(p_o, axis=-1, keepdims=True)
                acc = jnp.dot(p_o.astype(BF16), v_ref[0:r0, :], preferred_element_type=F32)
            p_d = jnp.exp(s_d - m)
            l_d = jnp.sum(p_d, axis=-1, keepdims=True)
            acc_d = jnp.dot(p_d.astype(BF16), v_ref[r0:r0 + blk, :], preferred_element_type=F32)
            if qi > 0:
                l = l + l_d
                acc = acc + acc_d
            else:
                l, acc = l_d, acc_d
            outs.append(acc * (1.0 / l))
        o_ref[r0:r0 + blk, :] = jnp.where(lane < MLA_V, outs[0], outs[1]).astype(o_ref.dtype)


def mla_attention(q, k, v, bsz, seq):
    blk = min(ATT_BLOCK, seq)
    n_pairs = MLA_HEADS // 2
    est = 12 * seq * LANES * 2 + 6 * blk * seq * 4
    return pl.pallas_call(
        functools.partial(_mla_attn_kernel, blk=blk, n_blocks=seq // blk),
        grid=(bsz, n_pairs),
        in_specs=[
            pl.BlockSpec((None, 2, seq, LANES), lambda b, h: (b, h, 0, 0)),
            pl.BlockSpec((None, 2, seq, LANES), lambda b, h: (b, h, 0, 0)),
            pl.BlockSpec((None, None, seq, LANES), lambda b, h: (b, h, 0, 0)),
        ],
        out_specs=pl.BlockSpec((seq, LANES), lambda b, h: (0, b * n_pairs + h)),
        out_shape=jax.ShapeDtypeStruct((seq, bsz * D_MODEL), BF16),
        compiler_params=_params(("parallel", "parallel"), est),
        name="mla_attention",
    )(q, k, v)


def _final_norm_kernel(x_ref, g_ref, o_ref):
    o_ref[...] = _rms(x_ref[...], g_ref[...])


def final_norm(x, g, bsz, seq):
    d = D_MODEL
    tl = min(ROW_TILE, seq)
    return pl.pallas_call(
        _final_norm_kernel,
        grid=(seq // tl, bsz),
        in_specs=[_x_spec(False, tl, d), pl.BlockSpec((1, d), lambda i, b: (0, 0))],
        out_specs=pl.BlockSpec((None, tl, d), lambda i, b: (b, i, 0)),
        out_shape=jax.ShapeDtypeStruct((bsz, seq, d), F32),
        compiler_params=_params(("parallel", "parallel"), 8 * tl * d * 4),
        name="final_norm",
    )(x, g)


def _row(v):
    return v.reshape(1, -1).astype(F32)


def _sb_qkv_weight(w_qkv):
    scale = SB_HEAD_DIM ** -0.5
    col_scale = jnp.concatenate([jnp.full((D_MODEL,), scale, F32), jnp.ones((2 * D_MODEL,), F32)])
    return (w_qkv * col_scale[None, :]).astype(BF16)


def _s5_block_diag(b_re, b_im, c_re, c_im):
    per = S5_SLAB // S5_GROUP
    n_slabs = S5_GROUPS // per
    eye = jnp.eye(per, dtype=F32)

    def b_blk(b):
        b4 = b.reshape(n_slabs, per, S5_STATE, S5_GROUP)
        return jnp.einsum('sgpc,gh->sgchp', b4, eye).reshape(n_slabs, per * S5_GROUP, per * S5_STATE)

    def c_blk(c):
        c4 = c.reshape(n_slabs, per, S5_GROUP, S5_STATE)
        return jnp.einsum('sgcp,gh->sgphc', c4, eye).reshape(n_slabs, per * S5_STATE, per * S5_GROUP)

    return b_blk(b_re), b_blk(b_im), c_blk(c_re).astype(BF16), c_blk(c_im).astype(BF16)


def _rotate_half_cols(w):
    half = w.shape[-1] // 2
    return jnp.concatenate([-w[..., half:], w[..., :half]], axis=-1)


def _mla_weights(w_in, w_q_b, w_kv_b):
    base = MLA_Q_RANK + MLA_KV_RANK
    w_kr = w_in[:, base:]
    pad = jnp.zeros((D_MODEL, LANES - MLA_ROPE), F32)
    w_in2 = jnp.concatenate([w_in[:, :base], w_kr, pad, _rotate_half_cols(w_kr), pad], axis=-1).astype(BF16)
    wq = w_q_b.reshape(MLA_Q_RANK, MLA_HEADS, MLA_NOPE + MLA_ROPE)
    q_nope, q_rope = wq[..., :MLA_NOPE], wq[..., MLA_NOPE:]
    zq = jnp.zeros((MLA_Q_RANK, MLA_HEADS, MLA_ROPE), F32)
    wa = jnp.concatenate([q_rope, zq, q_nope], axis=-1).reshape(MLA_Q_RANK, MLA_HEADS * LANES).astype(BF16)
    wb = jnp.concatenate([_rotate_half_cols(q_rope), zq, jnp.zeros_like(q_nope)], axis=-1)
    wb = wb.reshape(MLA_Q_RANK, MLA_HEADS * LANES).astype(BF16)
    wkv = w_kv_b.reshape(MLA_KV_RANK, MLA_HEADS, MLA_NOPE + MLA_V)
    k_nope, v_w = wkv[..., :MLA_NOPE], wkv[..., MLA_NOPE:]
    zk = jnp.zeros((MLA_KV_RANK, MLA_HEADS, 2 * MLA_ROPE), F32)
    wk_top = jnp.concatenate([zk, k_nope], axis=-1).reshape(MLA_KV_RANK, MLA_HEADS * LANES)
    route = jnp.concatenate([jnp.eye(LANES, MLA_ROPE, dtype=F32), jnp.zeros((LANES, LANES - MLA_ROPE), F32)], axis=-1)
    wk_bot = jnp.tile(route, (1, MLA_HEADS))
    wk = jnp.concatenate([wk_top, wk_bot], axis=0).astype(BF16)
    wv = v_w.reshape(MLA_KV_RANK, MLA_HEADS * MLA_V).astype(BF16)
    return w_in2, wa, wb, wk, wv


def _rope_tables(seq, bsz):
    pos = jnp.arange(seq, dtype=F32)
    inv_freq = ROPE_THETA ** (-jnp.arange(0, MLA_ROPE, 2, dtype=F32) / MLA_ROPE)
    ang = pos[:, None] * inv_freq[None, :]
    zeros = jnp.zeros((seq, LANES - MLA_ROPE), F32)
    cos = jnp.concatenate([jnp.cos(ang), jnp.cos(ang), zeros], axis=-1)
    sin = jnp.concatenate([jnp.sin(ang), jnp.sin(ang), zeros], axis=-1)
    return cos, sin


def moe_layer(flat, g_ffn, w_router, b_router, w_gate_up, w_down):
    w_r = jnp.pad(w_router, ((0, 0), (0, LANES - N_EXPERTS)))
    b_r = jnp.pad(b_router, (0, LANES - N_EXPERTS)).reshape(1, LANES)
    gates = router(flat, g_ffn, w_r, b_r)
    return ffn(flat, g_ffn, w_gate_up.astype(BF16), w_down.astype(BF16), gates)


def kernel(x, norm_mix, norm_ffn, final_norm_g, sb_w_qkv, sb_w_o, s5_lambda_re, s5_lambda_im, s5_log_dt, s5_b_re,
           s5_b_im, s5_c_re, s5_c_im, s5_d, s5_w_glu, mla_w_in, mla_q_norm, mla_w_q_b, mla_kv_norm, mla_w_kv_b,
           mla_w_o, ffn_w_gate_up, ffn_w_down, moe_w_router, moe_b_router, moe_w_gate_up, moe_w_down):
    bsz, seq, d = x.shape
    t = bsz * seq
    blk = min(ATT_BLOCK, seq)
    tri = (jnp.arange(blk)[:, None] > jnp.arange(blk)[None, :]).astype(BF16)
    cos_tab, sin_tab = _rope_tables(seq, bsz)

    cur = x
    is_bld = True

    def wide(a):
        return a if a.shape[0] == seq else a.reshape(seq, bsz * d)

    for i in range(DEPTH):
        kind, slot = i % N_MIXERS, i // N_MIXERS
        g_mix = _row(norm_mix[i])
        if kind == 0:
            cur = cur if is_bld else wide(cur)
            qkv = norm_proj_blocks(cur, is_bld, bsz, seq, g_mix, _sb_qkv_weight(sb_w_qkv[slot]))
            o = sb_attention(qkv, bsz, seq, tri)
            cur = proj_residual(o, cur, is_bld, seq, bsz, sb_w_o[slot].astype(BF16))
        elif kind == 1:
            b_re_blk, b_im_blk, c_re_blk, c_im_blk = _s5_block_diag(
                s5_b_re[slot], s5_b_im[slot], s5_c_re[slot], s5_c_im[slot])
            log_dt_row = jnp.repeat(s5_log_dt[slot], S5_STATE).reshape(1, -1)
            a_re, a_im, bb_re, bb_im = s5_discretise(
                _row(s5_lambda_re[slot]), _row(s5_lambda_im[slot]), log_dt_row, b_re_blk, b_im_blk)
            if is_bld:
                cur = jnp.transpose(cur, (1, 0, 2))
            flat = cur.reshape(t, d)
            y = s5_mixer(flat, g_mix, a_re, a_im, bb_re, bb_im, c_re_blk, c_im_blk, _row(s5_d[slot]), bsz)
            cur = proj_residual(y, flat, False, t, 1, s5_w_glu[slot].astype(BF16), glu=True)
        else:
            if is_bld:
                cur = jnp.transpose(cur, (1, 0, 2))
            cur = wide(cur)
            w_in2, wa, wb, wk, wv = _mla_weights(mla_w_in[slot], mla_w_q_b[slot], mla_w_kv_b[slot])
            q, k, v = mla_proj(cur, bsz, seq, g_mix, w_in2, _row(mla_q_norm[slot]), _row(mla_kv_norm[slot]),
                               wa, wb, wk, wv, cos_tab, sin_tab)
            o = mla_attention(q, k, v, bsz, seq)
            cur = proj_residual(o, cur, False, seq, bsz, mla_w_o[slot].astype(BF16))
        is_bld = False
        g_ffn = _row(norm_ffn[i])
        if i % 2 == 0:
            cur = ffn(cur, g_ffn, ffn_w_gate_up[i // 2][None].astype(BF16), ffn_w_down[i // 2][None].astype(BF16))
        else:
            cur = moe_layer(cur, g_ffn, moe_w_router[i // 2], moe_b_router[i // 2], moe_w_gate_up[i // 2],
                            moe_w_down[i // 2])
    return final_norm(wide(cur), _row(final_norm_g), bsz, seq)
```

```python
import functools
import math

import jax
import jax.numpy as jnp
from jax import lax
from jax.experimental import pallas as pl
from jax.experimental.pallas import tpu as pltpu

F32 = jnp.float32
BF16 = jnp.bfloat16

D_MODEL = 1024
DEPTH = 4
N_MIXERS = 3
RMS_EPS = 1e-6
LOG2_E = 1.0 / math.log(2.0)
SB_HEADS = 16
SB_HEAD_DIM = D_MODEL // SB_HEADS
S5_GROUP = 16
S5_GROUPS = D_MODEL // S5_GROUP
S5_STATE = 64
MLA_HEADS = 16
MLA_NOPE = 64
MLA_ROPE = 32
MLA_V = 64
MLA_Q_RANK = 384
MLA_KV_RANK = 256
ROPE_THETA = 10000.0
D_FF = 2816
N_EXPERTS = 8
TOP_K = 2
D_FF_EXPERT = 1408

LANES = 128
SUBLANES = 8
MXU_DIM = 256
VMEM_BYTES_V7X = 64 * 1024 * 1024
VMEM_CAP = VMEM_BYTES_V7X - 8 * 1024 * 1024

ROW_TILE = 512
ATT_BLOCK = MXU_DIM
S5_STEPS = 64
S5_SLAB = MXU_DIM
S5_LANE_CHUNK = 512
FF_CHUNK = 1408
MOE_ROW_TILE = 1024
MOE_CAPACITY = 384


def _params(semantics, est_bytes):
    limit = int(min(VMEM_CAP, max(32 * 1024 * 1024, est_bytes * 5 // 4)))
    return pltpu.CompilerParams(dimension_semantics=semantics, vmem_limit_bytes=limit)


def _rms(xf, g):
    ms = jnp.mean(xf * xf, axis=-1, keepdims=True)
    return xf * lax.rsqrt(ms + RMS_EPS) * g


def _sigmoid(v):
    return 1.0 / (1.0 + jnp.exp(-v))


def _x_spec(x_is_bld, tl, d):
    if x_is_bld:
        return pl.BlockSpec((None, tl, d), lambda i, b: (b, i, 0))
    return pl.BlockSpec((tl, d), lambda i, b: (i, b))


def _norm_proj_blocks_kernel(x_ref, g_ref, w_ref, o_ref):
    h = _rms(x_ref[...], g_ref[...]).astype(BF16)
    nb = o_ref.shape[0]
    per = 4
    for c in range(nb // per):
        r = jnp.dot(h, w_ref[:, c * per * LANES:(c + 1) * per * LANES], preferred_element_type=F32)
        for jj in range(per):
            o_ref[c * per + jj] = r[:, jj * LANES:(jj + 1) * LANES].astype(BF16)


def norm_proj_blocks(x, x_is_bld, bsz, seq, g, w):
    d, n = w.shape
    nb = n // LANES
    tl = min(ROW_TILE, seq)
    est = 2 * tl * d * 4 + 2 * d * n * 2 + 2 * n * tl * 2 + 4 * tl * 512 * 4
    return pl.pallas_call(
        _norm_proj_blocks_kernel,
        grid=(seq // tl, bsz),
        in_specs=[
            _x_spec(x_is_bld, tl, d),
            pl.BlockSpec((1, d), lambda i, b: (0, 0)),
            pl.BlockSpec((d, n), lambda i, b: (0, 0)),
        ],
        out_specs=pl.BlockSpec((nb, None, tl, LANES), lambda i, b: (0, b, i, 0)),
        out_shape=jax.ShapeDtypeStruct((nb, bsz, seq, LANES), BF16),
        compiler_params=_params(("parallel", "parallel"), est),
        name="norm_proj_blocks",
    )(x, g, w)


def _sb_attn_kernel(q_ref, k_ref, v_ref, u_ref, o_ref, *, blk, n_blocks):
    lane = lax.broadcasted_iota(jnp.int32, (1, LANES), 1)
    row = lax.broadcasted_iota(jnp.int32, (blk, blk), 0)
    col = lax.broadcasted_iota(jnp.int32, (blk, blk), 1)
    strict = col < row
    u = u_ref[...]
    nt = (((1,), (1,)), ((), ()))

    def neg_log2_one_minus_beta(z):
        return jnp.maximum(z, 0.0) + jnp.log2(1.0 + jnp.exp2(-jnp.abs(z)))

    for qi in range(n_blocks):
        r0 = qi * blk
        q = q_ref[r0:r0 + blk, :]
        accs = []
        for hh in range(2):
            head_lanes = (lane >= SB_HEAD_DIM) if hh else (lane < SB_HEAD_DIM)
            qh = jnp.where(head_lanes, q, jnp.zeros_like(q))
            z = lax.dot_general(qh, k_ref[r0:r0 + blk, :], nt, preferred_element_type=F32)
            nl = jnp.where(strict, neg_log2_one_minus_beta(z), 0.0)
            tail = jnp.dot(nl.astype(BF16), u, preferred_element_type=F32)
            w = jnp.where(strict, jnp.exp2((z - nl) - tail), 0.0)
            acc = jnp.dot(w.astype(BF16), v_ref[r0:r0 + blk, :], preferred_element_type=F32)
            run = jnp.sum(nl, axis=-1, keepdims=True)
            if qi > 0:
                z_off = lax.dot_general(qh, k_ref[0:r0, :], nt, preferred_element_type=F32)
                for j in range(qi - 1, -1, -1):
                    z = z_off[:, j * blk:(j + 1) * blk]
                    nl = neg_log2_one_minus_beta(z)
                    tail = jnp.dot(nl.astype(BF16), u, preferred_element_type=F32) + run
                    w = jnp.exp2((z - nl) - tail)
                    acc = acc + jnp.dot(w.astype(BF16), v_ref[j * blk:(j + 1) * blk, :], preferred_element_type=F32)
                    if j > 0:
                        run = run + jnp.sum(nl, axis=-1, keepdims=True)
            accs.append(acc)
        o_ref[r0:r0 + blk, :] = jnp.where(lane < SB_HEAD_DIM, accs[0], accs[1]).astype(o_ref.dtype)


def sb_attention(qkv, bsz, seq, tri):
    blk = min(ATT_BLOCK, seq)
    n_pairs = SB_HEADS // 2
    est = 8 * seq * LANES * 2 + 6 * blk * seq * 4
    return pl.pallas_call(
        functools.partial(_sb_attn_kernel, blk=blk, n_blocks=seq // blk),
        grid=(bsz, n_pairs),
        in_specs=[
            pl.BlockSpec((None, None, seq, LANES), lambda b, h: (h, b, 0, 0)),
            pl.BlockSpec((None, None, seq, LANES), lambda b, h: (n_pairs + h, b, 0, 0)),
            pl.BlockSpec((None, None, seq, LANES), lambda b, h: (2 * n_pairs + h, b, 0, 0)),
            pl.BlockSpec((blk, blk), lambda b, h: (0, 0)),
        ],
        out_specs=pl.BlockSpec((seq, LANES), lambda b, h: (0, b * n_pairs + h)),
        out_shape=jax.ShapeDtypeStruct((seq, bsz * D_MODEL), BF16),
        compiler_params=_params(("parallel", "parallel"), est),
        name="sb_attention",
    )(qkv, qkv, qkv, tri)


def _proj_residual_kernel(a_ref, x_ref, w_ref, o_ref, *, glu):
    r = jnp.dot(a_ref[...], w_ref[...], preferred_element_type=F32)
    if glu:
        n = o_ref.shape[-1]
        r = r[:, :n] * _sigmoid(r[:, n:])
    o_ref[...] = x_ref[...] + r


def proj_residual(a, x, x_is_bld, rows, ncol, w, glu=False):
    k, n_w = w.shape
    n = n_w // 2 if glu else n_w
    tl = min(ROW_TILE, rows)
    est = 2 * tl * k * 2 + 4 * tl * n * 4 + 2 * k * n_w * 2 + 2 * tl * n_w * 4
    return pl.pallas_call(
        functools.partial(_proj_residual_kernel, glu=glu),
        grid=(rows // tl, ncol),
        in_specs=[
            pl.BlockSpec((tl, k), lambda i, b: (i, b)),
            _x_spec(x_is_bld, tl, n),
            pl.BlockSpec((k, n_w), lambda i, b: (0, 0)),
        ],
        out_specs=pl.BlockSpec((tl, n), lambda i, b: (i, b)),
        out_shape=jax.ShapeDtypeStruct((rows, ncol * n), F32),
        compiler_params=_params(("parallel", "parallel"), est),
        name="proj_residual_glu" if glu else "proj_residual",
    )(a, x, w)


def _ffn_kernel(x_ref, g_ref, wg_ref, wu_ref, wd_ref, o_ref, h_ref):
    j = pl.program_id(2)

    @pl.when(j == 0)
    def _():
        xf = x_ref[...]
        h_ref[...] = _rms(xf, g_ref[...]).astype(BF16)
        o_ref[...] = xf

    h = h_ref[...]
    gv = jnp.dot(h, wg_ref[...], preferred_element_type=F32)
    uv = jnp.dot(h, wu_ref[...], preferred_element_type=F32)
    act = gv * _sigmoid(gv) * uv
    o_ref[...] += jnp.dot(act.astype(BF16), wd_ref[...], preferred_element_type=F32)


def ffn(x, g, w_gate_up, w_down):
    f, d = w_down.shape
    rows, ncol = x.shape[0], x.shape[1] // d
    tf = FF_CHUNK
    n_chunks = f // tf
    tm = min(ROW_TILE, rows)
    est = 4 * tm * d * 4 + tm * d * 2 + 6 * d * tf * 2 + 3 * tm * tf * 4
    return pl.pallas_call(
        _ffn_kernel,
        grid=(rows // tm, ncol, n_chunks),
        in_specs=[
            pl.BlockSpec((tm, d), lambda i, c, j: (i, c)),
            pl.BlockSpec((1, d), lambda i, c, j: (0, 0)),
            pl.BlockSpec((d, tf), lambda i, c, j: (0, j)),
            pl.BlockSpec((d, tf), lambda i, c, j: (0, n_chunks + j)),
            pl.BlockSpec((tf, d), lambda i, c, j: (j, 0)),
        ],
        out_specs=pl.BlockSpec((tm, d), lambda i, c, j: (i, c)),
        out_shape=jax.ShapeDtypeStruct(x.shape, F32),
        scratch_shapes=[pltpu.VMEM((tm, d), BF16)],
        compiler_params=_params(("parallel", "parallel", "arbitrary"), est),
        name="ffn",
    )(x, g, w_gate_up, w_gate_up, w_down)


def _moe_ffn_kernel(x_ref, g_ref, gates_ref, sel_ref, wg_ref, wu_ref, wd_ref, o_ref, h_ref, rank_ref, *, cap):
    e = pl.program_id(2)
    tm, d = x_ref.shape
    lane = lax.broadcasted_iota(jnp.int32, (1, LANES), 1)

    @pl.when(e == 0)
    def _():
        xf = x_ref[...]
        h_ref[...] = _rms(xf, g_ref[...]).astype(BF16)
        o_ref[...] = xf
        r_i = lax.broadcasted_iota(jnp.int32, (tm, tm), 0)
        c_i = lax.broadcasted_iota(jnp.int32, (tm, tm), 1)
        earlier = jnp.where(c_i < r_i, 1.0, 0.0).astype(BF16)
        rank_ref[...] = jnp.dot(earlier, sel_ref[...].astype(BF16), preferred_element_type=F32)

    mine = lane == e
    sel_e = jnp.sum(jnp.where(mine, sel_ref[...], 0.0), axis=-1, keepdims=True)
    rank_e = jnp.sum(jnp.where(mine, rank_ref[...], 0.0), axis=-1, keepdims=True)
    gate_e = jnp.sum(jnp.where(mine, gates_ref[...], 0.0), axis=-1, keepdims=True)
    count = jnp.sum(sel_e).astype(jnp.int32)
    n_pass = (count + (cap - 1)) // cap
    slot = lax.broadcasted_iota(jnp.int32, (1, cap), 1).astype(F32)
    half = d // 2

    def one_pass(p, carry):
        first = (p * cap).astype(F32)
        pick = jnp.where((rank_e - first == slot) & (sel_e > 0.5), 1.0, 0.0).astype(BF16)
        packed = lax.dot_general(pick, h_ref[...], (((0,), (0,)), ((), ())), preferred_element_type=F32)
        packed = packed.astype(BF16)
        gv = jnp.dot(packed, wg_ref[...], preferred_element_type=F32)
        uv = jnp.dot(packed, wu_ref[...], preferred_element_type=F32)
        act = (gv * _sigmoid(gv) * uv).astype(BF16)
        y = jnp.dot(act, wd_ref[...], preferred_element_type=F32).astype(BF16)
        for c0 in (0, half):
            spread = jnp.dot(pick, y[:, c0:c0 + half], preferred_element_type=F32)
            o_ref[:, c0:c0 + half] += gate_e * spread
        return carry

    lax.fori_loop(0, n_pass, one_pass, 0)


def moe_ffn(x, g, w_gate_up, w_down, gates, sel):
    n_e, f, d = w_down.shape
    rows, ncol = x.shape[0], x.shape[1] // d
    tm = min(MOE_ROW_TILE, rows)
    cap = min(MOE_CAPACITY, tm)
    est = (4 * tm * d * 4 + tm * d * 2 + 6 * d * f * 2 + 5 * tm * LANES * 4
           + 2 * tm * tm * 2 + 3 * cap * f * 4 + 2 * cap * d * 4 + 2 * tm * d * 4)
    return pl.pallas_call(
        functools.partial(_moe_ffn_kernel, cap=cap),
        grid=(rows // tm, ncol, n_e),
        in_specs=[
            pl.BlockSpec((tm, d), lambda i, c, e: (i, c)),
            pl.BlockSpec((1, d), lambda i, c, e: (0, 0)),
            pl.BlockSpec((tm, LANES), lambda i, c, e: (i, c)),
            pl.BlockSpec((tm, LANES), lambda i, c, e: (i, c)),
            pl.BlockSpec((None, d, f), lambda i, c, e: (e, 0, 0)),
            pl.BlockSpec((None, d, f), lambda i, c, e: (e, 0, 1)),
            pl.BlockSpec((None, f, d), lambda i, c, e: (e, 0, 0)),
        ],
        out_specs=pl.BlockSpec((tm, d), lambda i, c, e: (i, c)),
        out_shape=jax.ShapeDtypeStruct(x.shape, F32),
        scratch_shapes=[pltpu.VMEM((tm, d), BF16), pltpu.VMEM((tm, LANES), F32)],
        compiler_params=_params(("parallel", "parallel", "arbitrary"), est),
        name="moe_ffn",
    )(x, g, gates, sel, w_gate_up, w_gate_up, w_down)


def _router_kernel(x_ref, g_ref, w_ref, b_ref, o_ref, sel_ref):
    h = _rms(x_ref[...], g_ref[...])
    logits = jnp.dot(h, w_ref[...], preferred_element_type=F32, precision=lax.Precision.HIGHEST) + b_ref[...]
    lane = lax.broadcasted_iota(jnp.int32, logits.shape, 1).astype(F32)
    neg = jnp.float32(-jnp.inf)
    logits = jnp.where(lane < N_EXPERTS, logits, neg)
    m1 = jnp.max(logits, axis=-1, keepdims=True)
    i1 = jnp.min(jnp.where(logits == m1, lane, float(LANES)), axis=-1, keepdims=True)
    rest = jnp.where(lane == i1, neg, logits)
    m2 = jnp.max(rest, axis=-1, keepdims=True)
    i2 = jnp.min(jnp.where(rest == m2, lane, float(LANES)), axis=-1, keepdims=True)
    e2 = jnp.exp(m2 - m1)
    w1 = 1.0 / (1.0 + e2)
    w2 = e2 * w1
    o_ref[...] = jnp.where(lane == i1, w1, jnp.where(lane == i2, w2, 0.0))
    sel_ref[...] = jnp.where((lane == i1) | (lane == i2), 1.0, 0.0)


def router(x, g, w_pad, b_pad):
    d = w_pad.shape[0]
    rows, ncol = x.shape[0], x.shape[1] // d
    tm = min(ROW_TILE, rows)
    est = 2 * tm * d * 4 + 2 * d * LANES * 4 + 4 * tm * LANES * 4 + 2 * tm * d * 4
    return pl.pallas_call(
        _router_kernel,
        grid=(rows // tm, ncol),
        in_specs=[
            pl.BlockSpec((tm, d), lambda i, c: (i, c)),
            pl.BlockSpec((1, d), lambda i, c: (0, 0)),
            pl.BlockSpec((d, LANES), lambda i, c: (0, 0)),
            pl.BlockSpec((1, LANES), lambda i, c: (0, 0)),
        ],
        out_specs=(pl.BlockSpec((tm, LANES), lambda i, c: (i, c)), pl.BlockSpec((tm, LANES), lambda i, c: (i, c))),
        out_shape=(jax.ShapeDtypeStruct((rows, ncol * LANES), F32), jax.ShapeDtypeStruct((rows, ncol * LANES), F32)),
        compiler_params=_params(("parallel", "parallel"), est),
        name="router",
    )(x, g, w_pad, b_pad)


def _s5_discretise_kernel(lre_ref, lim_ref, ldt_ref, bre_ref, bim_ref, are_ref, aim_ref, bbre_ref, bbim_ref):
    lam_re = lre_ref[...]
    lam_im = lim_ref[...]
    dt = jnp.exp(ldt_ref[...])
    mag = jnp.exp(lam_re * dt)
    ang = lam_im * dt
    a_re = mag * jnp.cos(ang)
    a_im = mag * jnp.sin(ang)
    den = lam_re * lam_re + lam_im * lam_im
    num_re = a_re - 1.0
    f_re = (num_re * lam_re + a_im * lam_im) / den
    f_im = (a_im * lam_re - num_re * lam_im) / den
    are_ref[...] = a_re
    aim_ref[...] = a_im
    n_slabs, _, width = bre_ref.shape
    for s in range(n_slabs):
        fr = f_re[:, s * width:(s + 1) * width]
        fi = f_im[:, s * width:(s + 1) * width]
        br = bre_ref[s]
        bi = bim_ref[s]
        bbre_ref[s] = (fr * br - fi * bi).astype(BF16)
        bbim_ref[s] = (fr * bi + fi * br).astype(BF16)


def s5_discretise(lam_re_row, lam_im_row, log_dt_row, b_re_blk, b_im_blk):
    n_state = lam_re_row.shape[1]
    shp = b_re_blk.shape
    return pl.pallas_call(
        _s5_discretise_kernel,
        out_shape=(
            jax.ShapeDtypeStruct((1, n_state), F32),
            jax.ShapeDtypeStruct((1, n_state), F32),
            jax.ShapeDtypeStruct(shp, BF16),
            jax.ShapeDtypeStruct(shp, BF16),
        ),
        name="s5_discretise",
    )(lam_re_row, lam_im_row, log_dt_row, b_re_blk, b_im_blk)


def _gelu_tanh(y):
    c = math.sqrt(2.0 / math.pi)
    return 0.5 * y * (1.0 + jnp.tanh(c * (y + 0.044715 * (y * y * y))))


def _s5_kernel(x_ref, g_ref, are_ref, aim_ref, bbre_ref, bbim_ref, cre_ref, cim_ref, d_ref, o_ref,
               sre_ref, sim_ref, stre_ref, stim_ref, h_ref, *, bsz, steps):
    i = pl.program_id(0)

    @pl.when(i == 0)
    def _():
        stre_ref[...] = jnp.zeros_like(stre_ref)
        stim_ref[...] = jnp.zeros_like(stim_ref)

    h_ref[...] = _rms(x_ref[...], g_ref[...])
    n_slabs, slab, width = bbre_ref.shape
    for s in range(n_slabs):
        hs = h_ref[:, s * slab:(s + 1) * slab].astype(BF16)
        sre_ref[:, s * width:(s + 1) * width] = jnp.dot(hs, bbre_ref[s], preferred_element_type=F32)
        sim_ref[:, s * width:(s + 1) * width] = jnp.dot(hs, bbim_ref[s], preferred_element_type=F32)

    n_state = sre_ref.shape[1]
    cw = S5_LANE_CHUNK
    for c in range(n_state // cw):
        lanes = slice(c * cw, (c + 1) * cw)
        ar = jnp.broadcast_to(are_ref[:, lanes], (bsz, cw))
        ai = jnp.broadcast_to(aim_ref[:, lanes], (bsz, cw))

        def step(t, carry):
            xr, xi = carry
            r0 = pl.multiple_of(t * bsz, bsz)
            nr = ar * xr - ai * xi + sre_ref[pl.ds(r0, bsz), lanes]
            ni = ar * xi + ai * xr + sim_ref[pl.ds(r0, bsz), lanes]
            sre_ref[pl.ds(r0, bsz), lanes] = nr
            sim_ref[pl.ds(r0, bsz), lanes] = ni
            return nr, ni

        xr, xi = lax.fori_loop(0, steps, step, (stre_ref[:, lanes], stim_ref[:, lanes]), unroll=8)
        stre_ref[:, lanes] = xr
        stim_ref[:, lanes] = xi

    dsk = d_ref[...]
    for s in range(n_slabs):
        xr = sre_ref[:, s * width:(s + 1) * width].astype(BF16)
        xi = sim_ref[:, s * width:(s + 1) * width].astype(BF16)
        y = (jnp.dot(xr, cre_ref[s], preferred_element_type=F32)
             - jnp.dot(xi, cim_ref[s], preferred_element_type=F32))
        cols = slice(s * slab, (s + 1) * slab)
        y = y + dsk[:, cols] * h_ref[:, cols]
        o_ref[:, cols] = _gelu_tanh(y).astype(o_ref.dtype)


def s5_mixer(x, g, a_re, a_im, bb_re, bb_im, c_re_blk, c_im_blk, d_skip, bsz):
    t, d = x.shape
    seq = t // bsz
    steps = min(S5_STEPS, seq)
    rows = steps * bsz
    n_state = a_re.shape[1]
    wbytes = 2 * (bb_re.size + c_re_blk.size) * 2
    est = 2 * rows * d * 4 + 2 * rows * d * 2 + 2 * wbytes + 2 * rows * n_state * 4 + 4 * rows * 1024 * 4
    const3 = lambda i: (0, 0, 0)
    const2 = lambda i: (0, 0)
    return pl.pallas_call(
        functools.partial(_s5_kernel, bsz=bsz, steps=steps),
        grid=(seq // steps,),
        in_specs=[
            pl.BlockSpec((rows, d), lambda i: (i, 0)),
            pl.BlockSpec((1, d), const2),
            pl.BlockSpec((1, n_state), const2),
            pl.BlockSpec((1, n_state), const2),
            pl.BlockSpec(bb_re.shape, const3),
            pl.BlockSpec(bb_im.shape, const3),
            pl.BlockSpec(c_re_blk.shape, const3),
            pl.BlockSpec(c_im_blk.shape, const3),
            pl.BlockSpec((1, d), const2),
        ],
        out_specs=pl.BlockSpec((rows, d), lambda i: (i, 0)),
        out_shape=jax.ShapeDtypeStruct((t, d), BF16),
        scratch_shapes=[
            pltpu.VMEM((rows, n_state), F32),
            pltpu.VMEM((rows, n_state), F32),
            pltpu.VMEM((bsz, n_state), F32),
            pltpu.VMEM((bsz, n_state), F32),
            pltpu.VMEM((rows, d), F32),
        ],
        compiler_params=_params(("arbitrary",), est),
        name="s5_mixer",
    )(x, g, a_re, a_im, bb_re, bb_im, c_re_blk, c_im_blk, d_skip)


def _mla_proj_kernel(x_ref, g_ref, win_ref, qn_ref, kvn_ref, wa_ref, wb_ref, wk_ref, wv_ref, cos_ref, sin_ref,
                     q_ref, k_ref, v_ref, *, scale):
    h = _rms(x_ref[...], g_ref[...]).astype(BF16)
    c = jnp.dot(h, win_ref[...], preferred_element_type=F32)
    qc = _rms(c[:, :MLA_Q_RANK], qn_ref[...]).astype(BF16)
    kvc = _rms(c[:, MLA_Q_RANK:MLA_Q_RANK + MLA_KV_RANK], kvn_ref[...]).astype(BF16)
    cos = cos_ref[...]
    sin = sin_ref[...]
    base = MLA_Q_RANK + MLA_KV_RANK
    k_rope = c[:, base:base + LANES] * cos + c[:, base + LANES:base + 2 * LANES] * sin
    lane = lax.broadcasted_iota(jnp.int32, (1, LANES), 1)
    q_cos = (cos + jnp.where(lane >= 2 * MLA_ROPE, 1.0, 0.0)) * scale
    q_sin = sin * scale
    k_in = jnp.concatenate([kvc, k_rope.astype(BF16)], axis=-1)
    per = 4
    for cblk in range(MLA_HEADS // per):
        cols = slice(cblk * per * LANES, (cblk + 1) * per * LANES)
        qa = jnp.dot(qc, wa_ref[:, cols], preferred_element_type=F32)
        qb = jnp.dot(qc, wb_ref[:, cols], preferred_element_type=F32)
        kk = jnp.dot(k_in, wk_ref[:, cols], preferred_element_type=F32)
        for jj in range(per):
            sl = slice(jj * LANES, (jj + 1) * LANES)
            q_ref[cblk * per + jj] = (qa[:, sl] * q_cos + qb[:, sl] * q_sin).astype(BF16)
            k_ref[cblk * per + jj] = kk[:, sl].astype(BF16)
    vv = jnp.dot(kvc, wv_ref[...], preferred_element_type=F32)
    for jj in range(MLA_HEADS // 2):
        v_ref[jj] = vv[:, jj * LANES:(jj + 1) * LANES].astype(BF16)


def mla_proj(x, bsz, seq, g, w_in, q_norm, kv_norm, wa, wb, wk, wv, cos_tab, sin_tab):
    d = D_MODEL
    tl = min(ROW_TILE, seq)
    scale = float((MLA_NOPE + MLA_ROPE) ** -0.5) * LOG2_E
    wbytes = (w_in.size + wa.size + wb.size + wk.size + wv.size) * 2
    est = 2 * tl * d * 4 + 2 * wbytes + 2 * 5 * tl * 1024 * 2 + 8 * tl * 1024 * 4
    c2 = lambda i, b: (0, 0)
    nh = MLA_HEADS
    return pl.pallas_call(
        functools.partial(_mla_proj_kernel, scale=scale),
        grid=(seq // tl, bsz),
        in_specs=[
            _x_spec(False, tl, d),
            pl.BlockSpec((1, d), c2),
            pl.BlockSpec(w_in.shape, c2),
            pl.BlockSpec((1, MLA_Q_RANK), c2),
            pl.BlockSpec((1, MLA_KV_RANK), c2),
            pl.BlockSpec(wa.shape, c2),
            pl.BlockSpec(wb.shape, c2),
            pl.BlockSpec(wk.shape, c2),
            pl.BlockSpec(wv.shape, c2),
            pl.BlockSpec((tl, LANES), lambda i, b: (i, 0)),
            pl.BlockSpec((tl, LANES), lambda i, b: (i, 0)),
        ],
        out_specs=(
            pl.BlockSpec((None, nh, tl, LANES), lambda i, b: (b, 0, i, 0)),
            pl.BlockSpec((None, nh, tl, LANES), lambda i, b: (b, 0, i, 0)),
            pl.BlockSpec((None, nh // 2, tl, LANES), lambda i, b: (b, 0, i, 0)),
        ),
        out_shape=(
            jax.ShapeDtypeStruct((bsz, nh, seq, LANES), BF16),
            jax.ShapeDtypeStruct((bsz, nh, seq, LANES), BF16),
            jax.ShapeDtypeStruct((bsz, nh // 2, seq, LANES), BF16),
        ),
        compiler_params=_params(("parallel", "parallel"), est),
        name="mla_proj",
    )(x, g, w_in, q_norm, kv_norm, wa, wb, wk, wv, cos_tab, sin_tab)


def _mla_attn_kernel(q_ref, k_ref, v_ref, o_ref, *, blk, n_blocks):
    lane = lax.broadcasted_iota(jnp.int32, (1, LANES), 1)
    row = lax.broadcasted_iota(jnp.int32, (blk, blk), 0)
    col = lax.broadcasted_iota(jnp.int32, (blk, blk), 1)
    causal = col <= row
    nt = (((1,), (1,)), ((), ()))
    for qi in range(n_blocks):
        r0 = qi * blk
        outs = []
        for hh in range(2):
            qh = q_ref[hh, r0:r0 + blk, :]
            s_d = lax.dot_general(qh, k_ref[hh, r0:r0 + blk, :], nt, preferred_element_type=F32)
            s_d = jnp.where(causal, s_d, -jnp.inf)
            m = jnp.max(s_d, axis=-1, keepdims=True)
            if qi > 0:
                s_o = lax.dot_general(qh, k_ref[hh, 0:r0, :], nt, preferred_element_type=F32)
                m = jnp.maximum(m, jnp.max(s_o, axis=-1, keepdims=True))
                p_o = jnp.exp2(s_o - m)
                l = jnp.sum(p_o, axis=-1, keepdims=True)
                acc = jnp.dot(p_o.astype(BF16), v_ref[0:r0, :], preferred_element_type=F32)
            p_d = jnp.exp2(s_d - m)
            l_d = jnp.sum(p_d, axis=-1, keepdims=True)
            acc_d = jnp.dot(p_d.astype(BF16), v_ref[r0:r0 + blk, :], preferred_element_type=F32)
            if qi > 0:
                l = l + l_d
                acc = acc + acc_d
            else:
                l, acc = l_d, acc_d
            outs.append(acc * (1.0 / l))
        o_ref[r0:r0 + blk, :] = jnp.where(lane < MLA_V, outs[0], outs[1]).astype(o_ref.dtype)


def mla_attention(q, k, v, bsz, seq):
    blk = min(ATT_BLOCK, seq)
    n_pairs = MLA_HEADS // 2
    est = 12 * seq * LANES * 2 + 6 * blk * seq * 4
    return pl.pallas_call(
        functools.partial(_mla_attn_kernel, blk=blk, n_blocks=seq // blk),
        grid=(bsz, n_pairs),
        in_specs=[
            pl.BlockSpec((None, 2, seq, LANES), lambda b, h: (b, h, 0, 0)),
            pl.BlockSpec((None, 2, seq, LANES), lambda b, h: (b, h, 0, 0)),
            pl.BlockSpec((None, None, seq, LANES), lambda b, h: (b, h, 0, 0)),
        ],
        out_specs=pl.BlockSpec((seq, LANES), lambda b, h: (0, b * n_pairs + h)),
        out_shape=jax.ShapeDtypeStruct((seq, bsz * D_MODEL), BF16),
        compiler_params=_params(("parallel", "parallel"), est),
        name="mla_attention",
    )(q, k, v)


def _final_norm_kernel(x_ref, g_ref, o_ref):
    o_ref[...] = _rms(x_ref[...], g_ref[...])


def final_norm(x, g, bsz, seq):
    d = D_MODEL
    tl = min(ROW_TILE, seq)
    return pl.pallas_call(
        _final_norm_kernel,
        grid=(seq // tl, bsz),
        in_specs=[_x_spec(False, tl, d), pl.BlockSpec((1, d), lambda i, b: (0, 0))],
        out_specs=pl.BlockSpec((None, tl, d), lambda i, b: (b, i, 0)),
        out_shape=jax.ShapeDtypeStruct((bsz, seq, d), F32),
        compiler_params=_params(("parallel", "parallel"), 8 * tl * d * 4),
        name="final_norm",
    )(x, g)


def _row(v):
    return v.reshape(1, -1).astype(F32)


def _sb_qkv_weight(w_qkv):
    scale = SB_HEAD_DIM ** -0.5 * LOG2_E
    col_scale = jnp.concatenate([jnp.full((D_MODEL,), scale, F32), jnp.ones((2 * D_MODEL,), F32)])
    return (w_qkv * col_scale[None, :]).astype(BF16)


def _s5_block_diag(b_re, b_im, c_re, c_im):
    per = S5_SLAB // S5_GROUP
    n_slabs = S5_GROUPS // per
    eye = jnp.eye(per, dtype=F32)

    def b_blk(b):
        b4 = b.reshape(n_slabs, per, S5_STATE, S5_GROUP)
        return jnp.einsum('sgpc,gh->sgchp', b4, eye).reshape(n_slabs, per * S5_GROUP, per * S5_STATE)

    def c_blk(c):
        c4 = c.reshape(n_slabs, per, S5_GROUP, S5_STATE)
        return jnp.einsum('sgcp,gh->sgphc', c4, eye).reshape(n_slabs, per * S5_STATE, per * S5_GROUP)

    return b_blk(b_re), b_blk(b_im), c_blk(c_re).astype(BF16), c_blk(c_im).astype(BF16)


def _rotate_half_cols(w):
    half = w.shape[-1] // 2
    return jnp.concatenate([-w[..., half:], w[..., :half]], axis=-1)


def _mla_weights(w_in, w_q_b, w_kv_b):
    base = MLA_Q_RANK + MLA_KV_RANK
    w_kr = w_in[:, base:]
    pad = jnp.zeros((D_MODEL, LANES - MLA_ROPE), F32)
    w_in2 = jnp.concatenate([w_in[:, :base], w_kr, pad, _rotate_half_cols(w_kr), pad], axis=-1).astype(BF16)
    wq = w_q_b.reshape(MLA_Q_RANK, MLA_HEADS, MLA_NOPE + MLA_ROPE)
    q_nope, q_rope = wq[..., :MLA_NOPE], wq[..., MLA_NOPE:]
    zq = jnp.zeros((MLA_Q_RANK, MLA_HEADS, MLA_ROPE), F32)
    wa = jnp.concatenate([q_rope, zq, q_nope], axis=-1).reshape(MLA_Q_RANK, MLA_HEADS * LANES).astype(BF16)
    wb = jnp.concatenate([_rotate_half_cols(q_rope), zq, jnp.zeros_like(q_nope)], axis=-1)
    wb = wb.reshape(MLA_Q_RANK, MLA_HEADS * LANES).astype(BF16)
    wkv = w_kv_b.reshape(MLA_KV_RANK, MLA_HEADS, MLA_NOPE + MLA_V)
    k_nope, v_w = wkv[..., :MLA_NOPE], wkv[..., MLA_NOPE:]
    zk = jnp.zeros((MLA_KV_RANK, MLA_HEADS, 2 * MLA_ROPE), F32)
    wk_top = jnp.concatenate([zk, k_nope], axis=-1).reshape(MLA_KV_RANK, MLA_HEADS * LANES)
    route = jnp.concatenate([jnp.eye(LANES, MLA_ROPE, dtype=F32), jnp.zeros((LANES, LANES - MLA_ROPE), F32)], axis=-1)
    wk_bot = jnp.tile(route, (1, MLA_HEADS))
    wk = jnp.concatenate([wk_top, wk_bot], axis=0).astype(BF16)
    wv = v_w.reshape(MLA_KV_RANK, MLA_HEADS * MLA_V).astype(BF16)
    return w_in2, wa, wb, wk, wv


def _rope_tables(seq, bsz):
    pos = jnp.arange(seq, dtype=F32)
    inv_freq = ROPE_THETA ** (-jnp.arange(0, MLA_ROPE, 2, dtype=F32) / MLA_ROPE)
    ang = pos[:, None] * inv_freq[None, :]
    zeros = jnp.zeros((seq, LANES - MLA_ROPE), F32)
    cos = jnp.concatenate([jnp.cos(ang), jnp.cos(ang), zeros], axis=-1)
    sin = jnp.concatenate([jnp.sin(ang), jnp.sin(ang), zeros], axis=-1)
    return cos, sin


def moe_layer(flat, g_ffn, w_router, b_router, w_gate_up, w_down):
    w_r = jnp.pad(w_router, ((0, 0), (0, LANES - N_EXPERTS)))
    b_r = jnp.pad(b_router, (0, LANES - N_EXPERTS)).reshape(1, LANES)
    gates, sel = router(flat, g_ffn, w_r, b_r)
    return moe_ffn(flat, g_ffn, w_gate_up.astype(BF16), w_down.astype(BF16), gates, sel)


def kernel(x, norm_mix, norm_ffn, final_norm_g, sb_w_qkv, sb_w_o, s5_lambda_re, s5_lambda_im, s5_log_dt, s5_b_re,
           s5_b_im, s5_c_re, s5_c_im, s5_d, s5_w_glu, mla_w_in, mla_q_norm, mla_w_q_b, mla_kv_norm, mla_w_kv_b,
           mla_w_o, ffn_w_gate_up, ffn_w_down, moe_w_router, moe_b_router, moe_w_gate_up, moe_w_down):
    bsz, seq, d = x.shape
    t = bsz * seq
    blk = min(ATT_BLOCK, seq)
    tri = (jnp.arange(blk)[:, None] > jnp.arange(blk)[None, :]).astype(BF16)
    cos_tab, sin_tab = _rope_tables(seq, bsz)

    cur = x
    is_bld = True

    def wide(a):
        return a if a.shape[0] == seq else a.reshape(seq, bsz * d)

    for i in range(DEPTH):
        kind, slot = i % N_MIXERS, i // N_MIXERS
        g_mix = _row(norm_mix[i])
        if kind == 0:
            cur = cur if is_bld else wide(cur)
            qkv = norm_proj_blocks(cur, is_bld, bsz, seq, g_mix, _sb_qkv_weight(sb_w_qkv[slot]))
            o = sb_attention(qkv, bsz, seq, tri)
            cur = proj_residual(o, cur, is_bld, seq, bsz, sb_w_o[slot].astype(BF16))
        elif kind == 1:
            b_re_blk, b_im_blk, c_re_blk, c_im_blk = _s5_block_diag(
                s5_b_re[slot], s5_b_im[slot], s5_c_re[slot], s5_c_im[slot])
            log_dt_row = jnp.repeat(s5_log_dt[slot], S5_STATE).reshape(1, -1)
            a_re, a_im, bb_re, bb_im = s5_discretise(
                _row(s5_lambda_re[slot]), _row(s5_lambda_im[slot]), log_dt_row, b_re_blk, b_im_blk)
            if is_bld:
                cur = jnp.transpose(cur, (1, 0, 2))
            flat = cur.reshape(t, d)
            y = s5_mixer(flat, g_mix, a_re, a_im, bb_re, bb_im, c_re_blk, c_im_blk, _row(s5_d[slot]), bsz)
            cur = proj_residual(y, flat, False, t, 1, s5_w_glu[slot].astype(BF16), glu=True)
        else:
            if is_bld:
                cur = jnp.transpose(cur, (1, 0, 2))
            cur = wide(cur)
            w_in2, wa, wb, wk, wv = _mla_weights(mla_w_in[slot], mla_w_q_b[slot], mla_w_kv_b[slot])
            q, k, v = mla_proj(cur, bsz, seq, g_mix, w_in2, _row(mla_q_norm[slot]), _row(mla_kv_norm[slot]),
                               wa, wb, wk, wv, cos_tab, sin_tab)
            o = mla_attention(q, k, v, bsz, seq)
            cur = proj_residual(o, cur, False, seq, bsz, mla_w_o[slot].astype(BF16))
        is_bld = False
        g_ffn = _row(norm_ffn[i])
        if i % 2 == 0:
            cur = ffn(cur, g_ffn, ffn_w_gate_up[i // 2].astype(BF16), ffn_w_down[i // 2].astype(BF16))
        else:
            cur = moe_layer(cur, g_ffn, moe_w_router[i // 2], moe_b_router[i // 2], moe_w_gate_up[i // 2],
                            moe_w_down[i // 2])
    return final_norm(wide(cur), _row(final_norm_g), bsz, seq)
```

```python
import functools
import math

import jax
import jax.numpy as jnp
from jax import lax
from jax.experimental import pallas as pl
from jax.experimental.pallas import tpu as pltpu

F32 = jnp.float32
BF16 = jnp.bfloat16

D_MODEL = 1024
DEPTH = 4
N_MIXERS = 3
RMS_EPS = 1e-6
LOG2_E = 1.0 / math.log(2.0)
SB_HEADS = 16
SB_HEAD_DIM = D_MODEL // SB_HEADS
S5_GROUP = 16
S5_GROUPS = D_MODEL // S5_GROUP
S5_STATE = 64
MLA_HEADS = 16
MLA_NOPE = 64
MLA_ROPE = 32
MLA_V = 64
MLA_Q_RANK = 384
MLA_KV_RANK = 256
ROPE_THETA = 10000.0
D_FF = 2816
N_EXPERTS = 8
TOP_K = 2
D_FF_EXPERT = 1408

LANES = 128
SUBLANES = 8
MXU_DIM = 256
VMEM_BYTES_V7X = 64 * 1024 * 1024
VMEM_CAP = VMEM_BYTES_V7X - 8 * 1024 * 1024

ROW_TILE = 512
ATT_BLOCK = MXU_DIM
S5_STEPS = 64
S5_SLAB = MXU_DIM
S5_LANE_CHUNK = 512
FF_CHUNK = 1408
MOE_ROW_TILE = 1024
MOE_CAPACITY = 256
MOE_EXTRA = 128


def _params(semantics, est_bytes):
    limit = int(min(VMEM_CAP, max(32 * 1024 * 1024, est_bytes * 5 // 4)))
    return pltpu.CompilerParams(dimension_semantics=semantics, vmem_limit_bytes=limit)


def _rms(xf, g):
    ms = jnp.mean(xf * xf, axis=-1, keepdims=True)
    return xf * lax.rsqrt(ms + RMS_EPS) * g


def _sigmoid(v):
    return 1.0 / (1.0 + jnp.exp(-v))


def _x_spec(x_is_bld, tl, d):
    if x_is_bld:
        return pl.BlockSpec((None, tl, d), lambda i, b: (b, i, 0))
    return pl.BlockSpec((tl, d), lambda i, b: (i, b))


def _norm_proj_blocks_kernel(x_ref, g_ref, w_ref, o_ref):
    h = _rms(x_ref[...], g_ref[...]).astype(BF16)
    nb = o_ref.shape[0]
    per = 4
    for c in range(nb // per):
        r = jnp.dot(h, w_ref[:, c * per * LANES:(c + 1) * per * LANES], preferred_element_type=F32)
        for jj in range(per):
            o_ref[c * per + jj] = r[:, jj * LANES:(jj + 1) * LANES].astype(BF16)


def norm_proj_blocks(x, x_is_bld, bsz, seq, g, w):
    d, n = w.shape
    nb = n // LANES
    tl = min(ROW_TILE, seq)
    est = 2 * tl * d * 4 + 2 * d * n * 2 + 2 * n * tl * 2 + 4 * tl * 512 * 4
    return pl.pallas_call(
        _norm_proj_blocks_kernel,
        grid=(seq // tl, bsz),
        in_specs=[
            _x_spec(x_is_bld, tl, d),
            pl.BlockSpec((1, d), lambda i, b: (0, 0)),
            pl.BlockSpec((d, n), lambda i, b: (0, 0)),
        ],
        out_specs=pl.BlockSpec((nb, None, tl, LANES), lambda i, b: (0, b, i, 0)),
        out_shape=jax.ShapeDtypeStruct((nb, bsz, seq, LANES), BF16),
        compiler_params=_params(("parallel", "parallel"), est),
        name="norm_proj_blocks",
    )(x, g, w)


def _sb_attn_kernel(q_ref, k_ref, v_ref, u_ref, o_ref, *, blk, n_blocks):
    lane = lax.broadcasted_iota(jnp.int32, (1, LANES), 1)
    row = lax.broadcasted_iota(jnp.int32, (blk, blk), 0)
    col = lax.broadcasted_iota(jnp.int32, (blk, blk), 1)
    strict = col < row
    u = u_ref[...]
    nt = (((1,), (1,)), ((), ()))

    def neg_log2_one_minus_beta(z):
        return jnp.maximum(z, 0.0) + jnp.log2(1.0 + jnp.exp2(-jnp.abs(z)))

    for qi in range(n_blocks):
        r0 = qi * blk
        q = q_ref[r0:r0 + blk, :]
        accs = []
        for hh in range(2):
            head_lanes = (lane >= SB_HEAD_DIM) if hh else (lane < SB_HEAD_DIM)
            qh = jnp.where(head_lanes, q, jnp.zeros_like(q))
            z = lax.dot_general(qh, k_ref[r0:r0 + blk, :], nt, preferred_element_type=F32)
            nl = jnp.where(strict, neg_log2_one_minus_beta(z), 0.0)
            tail = jnp.dot(nl.astype(BF16), u, preferred_element_type=F32)
            w = jnp.where(strict, jnp.exp2((z - nl) - tail), 0.0)
            acc = jnp.dot(w.astype(BF16), v_ref[r0:r0 + blk, :], preferred_element_type=F32)
            run = jnp.sum(nl, axis=-1, keepdims=True)
            if qi > 0:
                z_off = lax.dot_general(qh, k_ref[0:r0, :], nt, preferred_element_type=F32)
                for j in range(qi - 1, -1, -1):
                    z = z_off[:, j * blk:(j + 1) * blk]
                    nl = neg_log2_one_minus_beta(z)
                    tail = jnp.dot(nl.astype(BF16), u, preferred_element_type=F32) + run
                    w = jnp.exp2((z - nl) - tail)
                    acc = acc + jnp.dot(w.astype(BF16), v_ref[j * blk:(j + 1) * blk, :], preferred_element_type=F32)
                    if j > 0:
                        run = run + jnp.sum(nl, axis=-1, keepdims=True)
            accs.append(acc)
        o_ref[r0:r0 + blk, :] = jnp.where(lane < SB_HEAD_DIM, accs[0], accs[1]).astype(o_ref.dtype)


def sb_attention(qkv, bsz, seq, tri):
    blk = min(ATT_BLOCK, seq)
    n_pairs = SB_HEADS // 2
    est = 8 * seq * LANES * 2 + 6 * blk * seq * 4
    return pl.pallas_call(
        functools.partial(_sb_attn_kernel, blk=blk, n_blocks=seq // blk),
        grid=(bsz, n_pairs),
        in_specs=[
            pl.BlockSpec((None, None, seq, LANES), lambda b, h: (h, b, 0, 0)),
            pl.BlockSpec((None, None, seq, LANES), lambda b, h: (n_pairs + h, b, 0, 0)),
            pl.BlockSpec((None, None, seq, LANES), lambda b, h: (2 * n_pairs + h, b, 0, 0)),
            pl.BlockSpec((blk, blk), lambda b, h: (0, 0)),
        ],
        out_specs=pl.BlockSpec((seq, LANES), lambda b, h: (0, b * n_pairs + h)),
        out_shape=jax.ShapeDtypeStruct((seq, bsz * D_MODEL), BF16),
        compiler_params=_params(("parallel", "parallel"), est),
        name="sb_attention",
    )(qkv, qkv, qkv, tri)


def _proj_residual_kernel(a_ref, x_ref, w_ref, o_ref, *, glu):
    r = jnp.dot(a_ref[...], w_ref[...], preferred_element_type=F32)
    if glu:
        n = o_ref.shape[-1]
        r = r[:, :n] * _sigmoid(r[:, n:])
    o_ref[...] = x_ref[...] + r


def proj_residual(a, x, x_is_bld, rows, ncol, w, glu=False):
    k, n_w = w.shape
    n = n_w // 2 if glu else n_w
    tl = min(ROW_TILE, rows)
    est = 2 * tl * k * 2 + 4 * tl * n * 4 + 2 * k * n_w * 2 + 2 * tl * n_w * 4
    return pl.pallas_call(
        functools.partial(_proj_residual_kernel, glu=glu),
        grid=(rows // tl, ncol),
        in_specs=[
            pl.BlockSpec((tl, k), lambda i, b: (i, b)),
            _x_spec(x_is_bld, tl, n),
            pl.BlockSpec((k, n_w), lambda i, b: (0, 0)),
        ],
        out_specs=pl.BlockSpec((tl, n), lambda i, b: (i, b)),
        out_shape=jax.ShapeDtypeStruct((rows, ncol * n), F32),
        compiler_params=_params(("parallel", "parallel"), est),
        name="proj_residual_glu" if glu else "proj_residual",
    )(a, x, w)


def _ffn_kernel(x_ref, g_ref, wg_ref, wu_ref, wd_ref, o_ref, h_ref):
    j = pl.program_id(2)

    @pl.when(j == 0)
    def _():
        xf = x_ref[...]
        h_ref[...] = _rms(xf, g_ref[...]).astype(BF16)
        o_ref[...] = xf

    h = h_ref[...]
    gv = jnp.dot(h, wg_ref[...], preferred_element_type=F32)
    uv = jnp.dot(h, wu_ref[...], preferred_element_type=F32)
    act = gv * _sigmoid(gv) * uv
    o_ref[...] += jnp.dot(act.astype(BF16), wd_ref[...], preferred_element_type=F32)


def ffn(x, g, w_gate_up, w_down):
    f, d = w_down.shape
    rows, ncol = x.shape[0], x.shape[1] // d
    tf = FF_CHUNK
    n_chunks = f // tf
    tm = min(ROW_TILE, rows)
    est = 4 * tm * d * 4 + tm * d * 2 + 6 * d * tf * 2 + 3 * tm * tf * 4
    return pl.pallas_call(
        _ffn_kernel,
        grid=(rows // tm, ncol, n_chunks),
        in_specs=[
            pl.BlockSpec((tm, d), lambda i, c, j: (i, c)),
            pl.BlockSpec((1, d), lambda i, c, j: (0, 0)),
            pl.BlockSpec((d, tf), lambda i, c, j: (0, j)),
            pl.BlockSpec((d, tf), lambda i, c, j: (0, n_chunks + j)),
            pl.BlockSpec((tf, d), lambda i, c, j: (j, 0)),
        ],
        out_specs=pl.BlockSpec((tm, d), lambda i, c, j: (i, c)),
        out_shape=jax.ShapeDtypeStruct(x.shape, F32),
        scratch_shapes=[pltpu.VMEM((tm, d), BF16)],
        compiler_params=_params(("parallel", "parallel", "arbitrary"), est),
        name="ffn",
    )(x, g, w_gate_up, w_gate_up, w_down)


def _moe_ffn_kernel(x_ref, g_ref, gates_ref, sel_ref, wg_ref, wu_ref, wd_ref, o_ref, h_ref, rank_ref, *, cap, extra):
    e = pl.program_id(2)
    tm, d = x_ref.shape
    lane = lax.broadcasted_iota(jnp.int32, (1, LANES), 1)

    @pl.when(e == 0)
    def _():
        xf = x_ref[...]
        h_ref[...] = _rms(xf, g_ref[...]).astype(BF16)
        o_ref[...] = xf
        r_i = lax.broadcasted_iota(jnp.int32, (tm, tm), 0)
        c_i = lax.broadcasted_iota(jnp.int32, (tm, tm), 1)
        earlier = jnp.where(c_i < r_i, 1.0, 0.0).astype(BF16)
        rank_ref[...] = jnp.dot(earlier, sel_ref[...].astype(BF16), preferred_element_type=F32)

    mine = lane == e
    sel_e = jnp.sum(jnp.where(mine, sel_ref[...], 0.0), axis=-1, keepdims=True)
    rank_e = jnp.sum(jnp.where(mine, rank_ref[...], 0.0), axis=-1, keepdims=True)
    gate_e = jnp.sum(jnp.where(mine, gates_ref[...], 0.0), axis=-1, keepdims=True)
    count = jnp.sum(sel_e).astype(jnp.int32)
    half = d // 2

    def run_slots(first, width):
        slot = lax.broadcasted_iota(jnp.int32, (1, width), 1).astype(F32)
        pick = jnp.where((rank_e - first == slot) & (sel_e > 0.5), 1.0, 0.0).astype(BF16)
        packed = lax.dot_general(pick, h_ref[...], (((0,), (0,)), ((), ())), preferred_element_type=F32)
        packed = packed.astype(BF16)
        gv = jnp.dot(packed, wg_ref[...], preferred_element_type=F32)
        uv = jnp.dot(packed, wu_ref[...], preferred_element_type=F32)
        act = (gv * _sigmoid(gv) * uv).astype(BF16)
        y = jnp.dot(act, wd_ref[...], preferred_element_type=F32).astype(BF16)
        for c0 in (0, half):
            spread = jnp.dot(pick, y[:, c0:c0 + half], preferred_element_type=F32)
            o_ref[:, c0:c0 + half] += gate_e * spread

    @pl.when(count > 0)
    def _():
        run_slots(jnp.float32(0.0), cap)

    def extra_pass(p, carry):
        run_slots((cap + p * extra).astype(F32), extra)
        return carry

    n_extra = (jnp.maximum(count - cap, 0) + (extra - 1)) // extra
    lax.fori_loop(0, n_extra, extra_pass, 0)


def moe_ffn(x, g, w_gate_up, w_down, gates, sel):
    n_e, f, d = w_down.shape
    rows, ncol = x.shape[0], x.shape[1] // d
    tm = min(MOE_ROW_TILE, rows)
    cap = min(MOE_CAPACITY, tm)
    extra = min(MOE_EXTRA, tm)
    est = (4 * tm * d * 4 + tm * d * 2 + 6 * d * f * 2 + 5 * tm * LANES * 4
           + 2 * tm * tm * 2 + 3 * cap * f * 4 + 2 * cap * d * 4 + 2 * tm * d * 4)
    return pl.pallas_call(
        functools.partial(_moe_ffn_kernel, cap=cap, extra=extra),
        grid=(rows // tm, ncol, n_e),
        in_specs=[
            pl.BlockSpec((tm, d), lambda i, c, e: (i, c)),
            pl.BlockSpec((1, d), lambda i, c, e: (0, 0)),
            pl.BlockSpec((tm, LANES), lambda i, c, e: (i, c)),
            pl.BlockSpec((tm, LANES), lambda i, c, e: (i, c)),
            pl.BlockSpec((None, d, f), lambda i, c, e: (e, 0, 0)),
            pl.BlockSpec((None, d, f), lambda i, c, e: (e, 0, 1)),
            pl.BlockSpec((None, f, d), lambda i, c, e: (e, 0, 0)),
        ],
        out_specs=pl.BlockSpec((tm, d), lambda i, c, e: (i, c)),
        out_shape=jax.ShapeDtypeStruct(x.shape, F32),
        scratch_shapes=[pltpu.VMEM((tm, d), BF16), pltpu.VMEM((tm, LANES), F32)],
        compiler_params=_params(("parallel", "parallel", "arbitrary"), est),
        name="moe_ffn",
    )(x, g, gates, sel, w_gate_up, w_gate_up, w_down)


def _router_kernel(x_ref, g_ref, w_ref, b_ref, o_ref, sel_ref):
    h = _rms(x_ref[...], g_ref[...])
    logits = jnp.dot(h, w_ref[...], preferred_element_type=F32, precision=lax.Precision.HIGHEST) + b_ref[...]
    lane = lax.broadcasted_iota(jnp.int32, logits.shape, 1).astype(F32)
    neg = jnp.float32(-jnp.inf)
    logits = jnp.where(lane < N_EXPERTS, logits, neg)
    m1 = jnp.max(logits, axis=-1, keepdims=True)
    i1 = jnp.min(jnp.where(logits == m1, lane, float(LANES)), axis=-1, keepdims=True)
    rest = jnp.where(lane == i1, neg, logits)
    m2 = jnp.max(rest, axis=-1, keepdims=True)
    i2 = jnp.min(jnp.where(rest == m2, lane, float(LANES)), axis=-1, keepdims=True)
    e2 = jnp.exp(m2 - m1)
    w1 = 1.0 / (1.0 + e2)
    w2 = e2 * w1
    o_ref[...] = jnp.where(lane == i1, w1, jnp.where(lane == i2, w2, 0.0))
    sel_ref[...] = jnp.where((lane == i1) | (lane == i2), 1.0, 0.0)


def router(x, g, w_pad, b_pad):
    d = w_pad.shape[0]
    rows, ncol = x.shape[0], x.shape[1] // d
    tm = min(ROW_TILE, rows)
    est = 2 * tm * d * 4 + 2 * d * LANES * 4 + 4 * tm * LANES * 4 + 2 * tm * d * 4
    return pl.pallas_call(
        _router_kernel,
        grid=(rows // tm, ncol),
        in_specs=[
            pl.BlockSpec((tm, d), lambda i, c: (i, c)),
            pl.BlockSpec((1, d), lambda i, c: (0, 0)),
            pl.BlockSpec((d, LANES), lambda i, c: (0, 0)),
            pl.BlockSpec((1, LANES), lambda i, c: (0, 0)),
        ],
        out_specs=(pl.BlockSpec((tm, LANES), lambda i, c: (i, c)), pl.BlockSpec((tm, LANES), lambda i, c: (i, c))),
        out_shape=(jax.ShapeDtypeStruct((rows, ncol * LANES), F32), jax.ShapeDtypeStruct((rows, ncol * LANES), F32)),
        compiler_params=_params(("parallel", "parallel"), est),
        name="router",
    )(x, g, w_pad, b_pad)


def _s5_discretise_kernel(lre_ref, lim_ref, ldt_ref, bre_ref, bim_ref, are_ref, aim_ref, bbre_ref, bbim_ref):
    lam_re = lre_ref[...]
    lam_im = lim_ref[...]
    dt = jnp.exp(ldt_ref[...])
    mag = jnp.exp(lam_re * dt)
    ang = lam_im * dt
    a_re = mag * jnp.cos(ang)
    a_im = mag * jnp.sin(ang)
    den = lam_re * lam_re + lam_im * lam_im
    num_re = a_re - 1.0
    f_re = (num_re * lam_re + a_im * lam_im) / den
    f_im = (a_im * lam_re - num_re * lam_im) / den
    are_ref[...] = a_re
    aim_ref[...] = a_im
    n_slabs, _, width = bre_ref.shape
    for s in range(n_slabs):
        fr = f_re[:, s * width:(s + 1) * width]
        fi = f_im[:, s * width:(s + 1) * width]
        br = bre_ref[s]
        bi = bim_ref[s]
        bbre_ref[s] = (fr * br - fi * bi).astype(BF16)
        bbim_ref[s] = (fr * bi + fi * br).astype(BF16)


def s5_discretise(lam_re_row, lam_im_row, log_dt_row, b_re_blk, b_im_blk):
    n_state = lam_re_row.shape[1]
    shp = b_re_blk.shape
    return pl.pallas_call(
        _s5_discretise_kernel,
        out_shape=(
            jax.ShapeDtypeStruct((1, n_state), F32),
            jax.ShapeDtypeStruct((1, n_state), F32),
            jax.ShapeDtypeStruct(shp, BF16),
            jax.ShapeDtypeStruct(shp, BF16),
        ),
        name="s5_discretise",
    )(lam_re_row, lam_im_row, log_dt_row, b_re_blk, b_im_blk)


def _gelu_tanh(y):
    c = math.sqrt(2.0 / math.pi)
    return 0.5 * y * (1.0 + jnp.tanh(c * (y + 0.044715 * (y * y * y))))


def _s5_kernel(x_ref, g_ref, are_ref, aim_ref, bbre_ref, bbim_ref, cre_ref, cim_ref, d_ref, o_ref,
               sre_ref, sim_ref, stre_ref, stim_ref, h_ref, *, bsz, steps):
    i = pl.program_id(0)

    @pl.when(i == 0)
    def _():
        stre_ref[...] = jnp.zeros_like(stre_ref)
        stim_ref[...] = jnp.zeros_like(stim_ref)

    h_ref[...] = _rms(x_ref[...], g_ref[...])
    n_slabs, slab, width = bbre_ref.shape
    for s in range(n_slabs):
        hs = h_ref[:, s * slab:(s + 1) * slab].astype(BF16)
        sre_ref[:, s * width:(s + 1) * width] = jnp.dot(hs, bbre_ref[s], preferred_element_type=F32)
        sim_ref[:, s * width:(s + 1) * width] = jnp.dot(hs, bbim_ref[s], preferred_element_type=F32)

    n_state = sre_ref.shape[1]
    cw = S5_LANE_CHUNK
    for c in range(n_state // cw):
        lanes = slice(c * cw, (c + 1) * cw)
        ar = jnp.broadcast_to(are_ref[:, lanes], (bsz, cw))
        ai = jnp.broadcast_to(aim_ref[:, lanes], (bsz, cw))

        def step(t, carry):
            xr, xi = carry
            r0 = pl.multiple_of(t * bsz, bsz)
            nr = ar * xr - ai * xi + sre_ref[pl.ds(r0, bsz), lanes]
            ni = ar * xi + ai * xr + sim_ref[pl.ds(r0, bsz), lanes]
            sre_ref[pl.ds(r0, bsz), lanes] = nr
            sim_ref[pl.ds(r0, bsz), lanes] = ni
            return nr, ni

        xr, xi = lax.fori_loop(0, steps, step, (stre_ref[:, lanes], stim_ref[:, lanes]), unroll=8)
        stre_ref[:, lanes] = xr
        stim_ref[:, lanes] = xi

    dsk = d_ref[...]
    for s in range(n_slabs):
        xr = sre_ref[:, s * width:(s + 1) * width].astype(BF16)
        xi = sim_ref[:, s * width:(s + 1) * width].astype(BF16)
        y = (jnp.dot(xr, cre_ref[s], preferred_element_type=F32)
             - jnp.dot(xi, cim_ref[s], preferred_element_type=F32))
        cols = slice(s * slab, (s + 1) * slab)
        y = y + dsk[:, cols] * h_ref[:, cols]
        o_ref[:, cols] = _gelu_tanh(y).astype(o_ref.dtype)


def s5_mixer(x, g, a_re, a_im, bb_re, bb_im, c_re_blk, c_im_blk, d_skip, bsz):
    t, d = x.shape
    seq = t // bsz
    steps = min(S5_STEPS, seq)
    rows = steps * bsz
    n_state = a_re.shape[1]
    wbytes = 2 * (bb_re.size + c_re_blk.size) * 2
    est = 2 * rows * d * 4 + 2 * rows * d * 2 + 2 * wbytes + 2 * rows * n_state * 4 + 4 * rows * 1024 * 4
    const3 = lambda i: (0, 0, 0)
    const2 = lambda i: (0, 0)
    return pl.pallas_call(
        functools.partial(_s5_kernel, bsz=bsz, steps=steps),
        grid=(seq // steps,),
        in_specs=[
            pl.BlockSpec((rows, d), lambda i: (i, 0)),
            pl.BlockSpec((1, d), const2),
            pl.BlockSpec((1, n_state), const2),
            pl.BlockSpec((1, n_state), const2),
            pl.BlockSpec(bb_re.shape, const3),
            pl.BlockSpec(bb_im.shape, const3),
            pl.BlockSpec(c_re_blk.shape, const3),
            pl.BlockSpec(c_im_blk.shape, const3),
            pl.BlockSpec((1, d), const2),
        ],
        out_specs=pl.BlockSpec((rows, d), lambda i: (i, 0)),
        out_shape=jax.ShapeDtypeStruct((t, d), BF16),
        scratch_shapes=[
            pltpu.VMEM((rows, n_state), F32),
            pltpu.VMEM((rows, n_state), F32),
            pltpu.VMEM((bsz, n_state), F32),
            pltpu.VMEM((bsz, n_state), F32),
            pltpu.VMEM((rows, d), F32),
        ],
        compiler_params=_params(("arbitrary",), est),
        name="s5_mixer",
    )(x, g, a_re, a_im, bb_re, bb_im, c_re_blk, c_im_blk, d_skip)


def _mla_proj_kernel(x_ref, g_ref, win_ref, qn_ref, kvn_ref, wa_ref, wb_ref, wk_ref, wv_ref, cos_ref, sin_ref,
                     q_ref, k_ref, v_ref, *, scale):
    h = _rms(x_ref[...], g_ref[...]).astype(BF16)
    c = jnp.dot(h, win_ref[...], preferred_element_type=F32)
    qc = _rms(c[:, :MLA_Q_RANK], qn_ref[...]).astype(BF16)
    kvc = _rms(c[:, MLA_Q_RANK:MLA_Q_RANK + MLA_KV_RANK], kvn_ref[...]).astype(BF16)
    cos = cos_ref[...]
    sin = sin_ref[...]
    base = MLA_Q_RANK + MLA_KV_RANK
    k_rope = c[:, base:base + LANES] * cos + c[:, base + LANES:base + 2 * LANES] * sin
    lane = lax.broadcasted_iota(jnp.int32, (1, LANES), 1)
    q_cos = (cos + jnp.where(lane >= 2 * MLA_ROPE, 1.0, 0.0)) * scale
    q_sin = sin * scale
    k_in = jnp.concatenate([kvc, k_rope.astype(BF16)], axis=-1)
    per = 4
    for cblk in range(MLA_HEADS // per):
        cols = slice(cblk * per * LANES, (cblk + 1) * per * LANES)
        qa = jnp.dot(qc, wa_ref[:, cols], preferred_element_type=F32)
        qb = jnp.dot(qc, wb_ref[:, cols], preferred_element_type=F32)
        kk = jnp.dot(k_in, wk_ref[:, cols], preferred_element_type=F32)
        for jj in range(per):
            sl = slice(jj * LANES, (jj + 1) * LANES)
            q_ref[cblk * per + jj] = (qa[:, sl] * q_cos + qb[:, sl] * q_sin).astype(BF16)
            k_ref[cblk * per + jj] = kk[:, sl].astype(BF16)
    vv = jnp.dot(kvc, wv_ref[...], preferred_element_type=F32)
    for jj in range(MLA_HEADS // 2):
        v_ref[jj] = vv[:, jj * LANES:(jj + 1) * LANES].astype(BF16)


def mla_proj(x, bsz, seq, g, w_in, q_norm, kv_norm, wa, wb, wk, wv, cos_tab, sin_tab):
    d = D_MODEL
    tl = min(ROW_TILE, seq)
    scale = float((MLA_NOPE + MLA_ROPE) ** -0.5) * LOG2_E
    wbytes = (w_in.size + wa.size + wb.size + wk.size + wv.size) * 2
    est = 2 * tl * d * 4 + 2 * wbytes + 2 * 5 * tl * 1024 * 2 + 8 * tl * 1024 * 4
    c2 = lambda i, b: (0, 0)
    nh = MLA_HEADS
    return pl.pallas_call(
        functools.partial(_mla_proj_kernel, scale=scale),
        grid=(seq // tl, bsz),
        in_specs=[
            _x_spec(False, tl, d),
            pl.BlockSpec((1, d), c2),
            pl.BlockSpec(w_in.shape, c2),
            pl.BlockSpec((1, MLA_Q_RANK), c2),
            pl.BlockSpec((1, MLA_KV_RANK), c2),
            pl.BlockSpec(wa.shape, c2),
            pl.BlockSpec(wb.shape, c2),
            pl.BlockSpec(wk.shape, c2),
            pl.BlockSpec(wv.shape, c2),
            pl.BlockSpec((tl, LANES), lambda i, b: (i, 0)),
            pl.BlockSpec((tl, LANES), lambda i, b: (i, 0)),
        ],
        out_specs=(
            pl.BlockSpec((None, nh, tl, LANES), lambda i, b: (b, 0, i, 0)),
            pl.BlockSpec((None, nh, tl, LANES), lambda i, b: (b, 0, i, 0)),
            pl.BlockSpec((None, nh // 2, tl, LANES), lambda i, b: (b, 0, i, 0)),
        ),
        out_shape=(
            jax.ShapeDtypeStruct((bsz, nh, seq, LANES), BF16),
            jax.ShapeDtypeStruct((bsz, nh, seq, LANES), BF16),
            jax.ShapeDtypeStruct((bsz, nh // 2, seq, LANES), BF16),
        ),
        compiler_params=_params(("parallel", "parallel"), est),
        name="mla_proj",
    )(x, g, w_in, q_norm, kv_norm, wa, wb, wk, wv, cos_tab, sin_tab)


def _mla_attn_kernel(q_ref, k_ref, v_ref, o_ref, *, blk, n_blocks):
    lane = lax.broadcasted_iota(jnp.int32, (1, LANES), 1)
    row = lax.broadcasted_iota(jnp.int32, (blk, blk), 0)
    col = lax.broadcasted_iota(jnp.int32, (blk, blk), 1)
    causal = col <= row
    nt = (((1,), (1,)), ((), ()))
    for qi in range(n_blocks):
        r0 = qi * blk
        outs = []
        for hh in range(2):
            qh = q_ref[hh, r0:r0 + blk, :]
            s_d = lax.dot_general(qh, k_ref[hh, r0:r0 + blk, :], nt, preferred_element_type=F32)
            s_d = jnp.where(causal, s_d, -jnp.inf)
            m = jnp.max(s_d, axis=-1, keepdims=True)
            if qi > 0:
                s_o = lax.dot_general(qh, k_ref[hh, 0:r0, :], nt, preferred_element_type=F32)
                m = jnp.maximum(m, jnp.max(s_o, axis=-1, keepdims=True))
                p_o = jnp.exp2(s_o - m)
                l = jnp.sum(p_o, axis=-1, keepdims=True)
                acc = jnp.dot(p_o.astype(BF16), v_ref[0:r0, :], preferred_element_type=F32)
            p_d = jnp.exp2(s_d - m)
            l_d = jnp.sum(p_d, axis=-1, keepdims=True)
            acc_d = jnp.dot(p_d.astype(BF16), v_ref[r0:r0 + blk, :], preferred_element_type=F32)
            if qi > 0:
                l = l + l_d
                acc = acc + acc_d
            else:
                l, acc = l_d, acc_d
            outs.append(acc * (1.0 / l))
        o_ref[r0:r0 + blk, :] = jnp.where(lane < MLA_V, outs[0], outs[1]).astype(o_ref.dtype)


def mla_attention(q, k, v, bsz, seq):
    blk = min(ATT_BLOCK, seq)
    n_pairs = MLA_HEADS // 2
    est = 12 * seq * LANES * 2 + 6 * blk * seq * 4
    return pl.pallas_call(
        functools.partial(_mla_attn_kernel, blk=blk, n_blocks=seq // blk),
        grid=(bsz, n_pairs),
        in_specs=[
            pl.BlockSpec((None, 2, seq, LANES), lambda b, h: (b, h, 0, 0)),
            pl.BlockSpec((None, 2, seq, LANES), lambda b, h: (b, h, 0, 0)),
            pl.BlockSpec((None, None, seq, LANES), lambda b, h: (b, h, 0, 0)),
        ],
        out_specs=pl.BlockSpec((seq, LANES), lambda b, h: (0, b * n_pairs + h)),
        out_shape=jax.ShapeDtypeStruct((seq, bsz * D_MODEL), BF16),
        compiler_params=_params(("parallel", "parallel"), est),
        name="mla_attention",
    )(q, k, v)


def _final_norm_kernel(x_ref, g_ref, o_ref):
    o_ref[...] = _rms(x_ref[...], g_ref[...])


def final_norm(x, g, bsz, seq):
    d = D_MODEL
    tl = min(ROW_TILE, seq)
    return pl.pallas_call(
        _final_norm_kernel,
        grid=(seq // tl, bsz),
        in_specs=[_x_spec(False, tl, d), pl.BlockSpec((1, d), lambda i, b: (0, 0))],
        out_specs=pl.BlockSpec((None, tl, d), lambda i, b: (b, i, 0)),
        out_shape=jax.ShapeDtypeStruct((bsz, seq, d), F32),
        compiler_params=_params(("parallel", "parallel"), 8 * tl * d * 4),
        name="final_norm",
    )(x, g)


def _row(v):
    return v.reshape(1, -1).astype(F32)


def _sb_qkv_weight(w_qkv):
    scale = SB_HEAD_DIM ** -0.5 * LOG2_E
    col_scale = jnp.concatenate([jnp.full((D_MODEL,), scale, F32), jnp.ones((2 * D_MODEL,), F32)])
    return (w_qkv * col_scale[None, :]).astype(BF16)


def _s5_block_diag(b_re, b_im, c_re, c_im):
    per = S5_SLAB // S5_GROUP
    n_slabs = S5_GROUPS // per
    eye = jnp.eye(per, dtype=F32)

    def b_blk(b):
        b4 = b.reshape(n_slabs, per, S5_STATE, S5_GROUP)
        return jnp.einsum('sgpc,gh->sgchp', b4, eye).reshape(n_slabs, per * S5_GROUP, per * S5_STATE)

    def c_blk(c):
        c4 = c.reshape(n_slabs, per, S5_GROUP, S5_STATE)
        return jnp.einsum('sgcp,gh->sgphc', c4, eye).reshape(n_slabs, per * S5_STATE, per * S5_GROUP)

    return b_blk(b_re), b_blk(b_im), c_blk(c_re).astype(BF16), c_blk(c_im).astype(BF16)


def _rotate_half_cols(w):
    half = w.shape[-1] // 2
    return jnp.concatenate([-w[..., half:], w[..., :half]], axis=-1)


def _mla_weights(w_in, w_q_b, w_kv_b):
    base = MLA_Q_RANK + MLA_KV_RANK
    w_kr = w_in[:, base:]
    pad = jnp.zeros((D_MODEL, LANES - MLA_ROPE), F32)
    w_in2 = jnp.concatenate([w_in[:, :base], w_kr, pad, _rotate_half_cols(w_kr), pad], axis=-1).astype(BF16)
    wq = w_q_b.reshape(MLA_Q_RANK, MLA_HEADS, MLA_NOPE + MLA_ROPE)
    q_nope, q_rope = wq[..., :MLA_NOPE], wq[..., MLA_NOPE:]
    zq = jnp.zeros((MLA_Q_RANK, MLA_HEADS, MLA_ROPE), F32)
    wa = jnp.concatenate([q_rope, zq, q_nope], axis=-1).reshape(MLA_Q_RANK, MLA_HEADS * LANES).astype(BF16)
    wb = jnp.concatenate([_rotate_half_cols(q_rope), zq, jnp.zeros_like(q_nope)], axis=-1)
    wb = wb.reshape(MLA_Q_RANK, MLA_HEADS * LANES).astype(BF16)
    wkv = w_kv_b.reshape(MLA_KV_RANK, MLA_HEADS, MLA_NOPE + MLA_V)
    k_nope, v_w = wkv[..., :MLA_NOPE], wkv[..., MLA_NOPE:]
    zk = jnp.zeros((MLA_KV_RANK, MLA_HEADS, 2 * MLA_ROPE), F32)
    wk_top = jnp.concatenate([zk, k_nope], axis=-1).reshape(MLA_KV_RANK, MLA_HEADS * LANES)
    route = jnp.concatenate([jnp.eye(LANES, MLA_ROPE, dtype=F32), jnp.zeros((LANES, LANES - MLA_ROPE), F32)], axis=-1)
    wk_bot = jnp.tile(route, (1, MLA_HEADS))
    wk = jnp.concatenate([wk_top, wk_bot], axis=0).astype(BF16)
    wv = v_w.reshape(MLA_KV_RANK, MLA_HEADS * MLA_V).astype(BF16)
    return w_in2, wa, wb, wk, wv


def _rope_tables(seq, bsz):
    pos = jnp.arange(seq, dtype=F32)
    inv_freq = ROPE_THETA ** (-jnp.arange(0, MLA_ROPE, 2, dtype=F32) / MLA_ROPE)
    ang = pos[:, None] * inv_freq[None, :]
    zeros = jnp.zeros((seq, LANES - MLA_ROPE), F32)
    cos = jnp.concatenate([jnp.cos(ang), jnp.cos(ang), zeros], axis=-1)
    sin = jnp.concatenate([jnp.sin(ang), jnp.sin(ang), zeros], axis=-1)
    return cos, sin


def moe_layer(flat, g_ffn, w_router, b_router, w_gate_up, w_down):
    w_r = jnp.pad(w_router, ((0, 0), (0, LANES - N_EXPERTS)))
    b_r = jnp.pad(b_router, (0, LANES - N_EXPERTS)).reshape(1, LANES)
    gates, sel = router(flat, g_ffn, w_r, b_r)
    return moe_ffn(flat, g_ffn, w_gate_up.astype(BF16), w_down.astype(BF16), gates, sel)


def kernel(x, norm_mix, norm_ffn, final_norm_g, sb_w_qkv, sb_w_o, s5_lambda_re, s5_lambda_im, s5_log_dt, s5_b_re,
           s5_b_im, s5_c_re, s5_c_im, s5_d, s5_w_glu, mla_w_in, mla_q_norm, mla_w_q_b, mla_kv_norm, mla_w_kv_b,
           mla_w_o, ffn_w_gate_up, ffn_w_down, moe_w_router, moe_b_router, moe_w_gate_up, moe_w_down):
    bsz, seq, d = x.shape
    t = bsz * seq
    blk = min(ATT_BLOCK, seq)
    tri = (jnp.arange(blk)[:, None] > jnp.arange(blk)[None, :]).astype(BF16)
    cos_tab, sin_tab = _rope_tables(seq, bsz)

    cur = x
    is_bld = True

    def wide(a):
        return a if a.shape[0] == seq else a.reshape(seq, bsz * d)

    for i in range(DEPTH):
        kind, slot = i % N_MIXERS, i // N_MIXERS
        g_mix = _row(norm_mix[i])
        if kind == 0:
            cur = cur if is_bld else wide(cur)
            qkv = norm_proj_blocks(cur, is_bld, bsz, seq, g_mix, _sb_qkv_weight(sb_w_qkv[slot]))
            o = sb_attention(qkv, bsz, seq, tri)
            cur = proj_residual(o, cur, is_bld, seq, bsz, sb_w_o[slot].astype(BF16))
        elif kind == 1:
            b_re_blk, b_im_blk, c_re_blk, c_im_blk = _s5_block_diag(
                s5_b_re[slot], s5_b_im[slot], s5_c_re[slot], s5_c_im[slot])
            log_dt_row = jnp.repeat(s5_log_dt[slot], S5_STATE).reshape(1, -1)
            a_re, a_im, bb_re, bb_im = s5_discretise(
                _row(s5_lambda_re[slot]), _row(s5_lambda_im[slot]), log_dt_row, b_re_blk, b_im_blk)
            if is_bld:
                cur = jnp.transpose(cur, (1, 0, 2))
            flat = cur.reshape(t, d)
            y = s5_mixer(flat, g_mix, a_re, a_im, bb_re, bb_im, c_re_blk, c_im_blk, _row(s5_d[slot]), bsz)
            cur = proj_residual(y, flat, False, t, 1, s5_w_glu[slot].astype(BF16), glu=True)
        else:
            if is_bld:
                cur = jnp.transpose(cur, (1, 0, 2))
            cur = wide(cur)
            w_in2, wa, wb, wk, wv = _mla_weights(mla_w_in[slot], mla_w_q_b[slot], mla_w_kv_b[slot])
            q, k, v = mla_proj(cur, bsz, seq, g_mix, w_in2, _row(mla_q_norm[slot]), _row(mla_kv_norm[slot]),
                               wa, wb, wk, wv, cos_tab, sin_tab)
            o = mla_attention(q, k, v, bsz, seq)
            cur = proj_residual(o, cur, False, seq, bsz, mla_w_o[slot].astype(BF16))
        is_bld = False
        g_ffn = _row(norm_ffn[i])
        if i % 2 == 0:
            cur = ffn(cur, g_ffn, ffn_w_gate_up[i // 2].astype(BF16), ffn_w_down[i // 2].astype(BF16))
        else:
            cur = moe_layer(cur, g_ffn, moe_w_router[i // 2], moe_b_router[i // 2], moe_w_gate_up[i // 2],
                            moe_w_down[i // 2])
    return final_norm(wide(cur), _row(final_norm_g), bsz, seq)
```

```python
import functools
import math

import jax
import jax.numpy as jnp
from jax import lax
from jax.experimental import pallas as pl
from jax.experimental.pallas import tpu as pltpu

F32 = jnp.float32
BF16 = jnp.bfloat16

D_MODEL = 1024
DEPTH = 4
N_MIXERS = 3
RMS_EPS = 1e-6
LOG2_E = 1.0 / math.log(2.0)
SB_HEADS = 16
SB_HEAD_DIM = D_MODEL // SB_HEADS
S5_GROUP = 16
S5_GROUPS = D_MODEL // S5_GROUP
S5_STATE = 64
MLA_HEADS = 16
MLA_NOPE = 64
MLA_ROPE = 32
MLA_V = 64
MLA_Q_RANK = 384
MLA_KV_RANK = 256
ROPE_THETA = 10000.0
D_FF = 2816
N_EXPERTS = 8
TOP_K = 2
D_FF_EXPERT = 1408

LANES = 128
SUBLANES = 8
MXU_DIM = 256
VMEM_BYTES_V7X = 64 * 1024 * 1024
VMEM_CAP = VMEM_BYTES_V7X - 8 * 1024 * 1024

ROW_TILE = 512
ATT_BLOCK = MXU_DIM
S5_STEPS = 64
S5_SLAB = MXU_DIM
S5_LANE_CHUNK = 512
FF_CHUNK = 1408
MOE_ROW_TILE = 1024
MOE_CAPACITY = 256
MOE_EXTRA = 128


def _params(semantics, est_bytes):
    limit = int(min(VMEM_CAP, max(32 * 1024 * 1024, est_bytes * 5 // 4)))
    return pltpu.CompilerParams(dimension_semantics=semantics, vmem_limit_bytes=limit)


def _rms(xf, g):
    ms = jnp.mean(xf * xf, axis=-1, keepdims=True)
    return xf * lax.rsqrt(ms + RMS_EPS) * g


def _sigmoid(v):
    return 1.0 / (1.0 + jnp.exp(-v))


def _x_spec(x_is_bld, tl, d):
    if x_is_bld:
        return pl.BlockSpec((None, tl, d), lambda i, b: (b, i, 0))
    return pl.BlockSpec((tl, d), lambda i, b: (i, b))


def _norm_proj_blocks_kernel(x_ref, g_ref, w_ref, o_ref):
    h = _rms(x_ref[...], g_ref[...]).astype(BF16)
    nb = o_ref.shape[0]
    per = 4
    for c in range(nb // per):
        r = jnp.dot(h, w_ref[:, c * per * LANES:(c + 1) * per * LANES], preferred_element_type=F32)
        for jj in range(per):
            o_ref[c * per + jj] = r[:, jj * LANES:(jj + 1) * LANES].astype(BF16)


def norm_proj_blocks(x, x_is_bld, bsz, seq, g, w):
    d, n = w.shape
    nb = n // LANES
    tl = min(ROW_TILE, seq)
    est = 2 * tl * d * 4 + 2 * d * n * 2 + 2 * n * tl * 2 + 4 * tl * 512 * 4
    return pl.pallas_call(
        _norm_proj_blocks_kernel,
        grid=(seq // tl, bsz),
        in_specs=[
            _x_spec(x_is_bld, tl, d),
            pl.BlockSpec((1, d), lambda i, b: (0, 0)),
            pl.BlockSpec((d, n), lambda i, b: (0, 0)),
        ],
        out_specs=pl.BlockSpec((nb, None, tl, LANES), lambda i, b: (0, b, i, 0)),
        out_shape=jax.ShapeDtypeStruct((nb, bsz, seq, LANES), BF16),
        compiler_params=_params(("parallel", "parallel"), est),
        name="norm_proj_blocks",
    )(x, g, w)


def _sb_attn_kernel(q_ref, k_ref, v_ref, u_ref, o_ref, *, blk, n_blocks):
    lane = lax.broadcasted_iota(jnp.int32, (1, LANES), 1)
    row = lax.broadcasted_iota(jnp.int32, (blk, blk), 0)
    col = lax.broadcasted_iota(jnp.int32, (blk, blk), 1)
    strict = col < row
    u = u_ref[...]
    nt = (((1,), (1,)), ((), ()))

    def neg_log2_one_minus_beta(z):
        return jnp.maximum(z, 0.0) + jnp.log2(1.0 + jnp.exp2(-jnp.abs(z)))

    v_pair = v_ref[...]
    v_heads = [jnp.where(lane < SB_HEAD_DIM, v_pair, jnp.zeros_like(v_pair)),
               jnp.where(lane >= SB_HEAD_DIM, v_pair, jnp.zeros_like(v_pair))]
    for qi in range(n_blocks):
        r0 = qi * blk
        q = q_ref[r0:r0 + blk, :]
        weights, vals = [], []
        for hh in range(2):
            head_lanes = (lane >= SB_HEAD_DIM) if hh else (lane < SB_HEAD_DIM)
            qh = jnp.where(head_lanes, q, jnp.zeros_like(q))
            z = lax.dot_general(qh, k_ref[r0:r0 + blk, :], nt, preferred_element_type=F32)
            nl = jnp.where(strict, neg_log2_one_minus_beta(z), 0.0)
            tail = jnp.dot(nl.astype(BF16), u, preferred_element_type=F32)
            w = jnp.where(strict, jnp.exp2((z - nl) - tail), 0.0)
            weights.append(w.astype(BF16))
            vals.append(v_heads[hh][r0:r0 + blk, :])
            run = jnp.sum(nl, axis=-1, keepdims=True)
            if qi > 0:
                z_off = lax.dot_general(qh, k_ref[0:r0, :], nt, preferred_element_type=F32)
                for j in range(qi - 1, -1, -1):
                    z = z_off[:, j * blk:(j + 1) * blk]
                    nl = neg_log2_one_minus_beta(z)
                    tail = jnp.dot(nl.astype(BF16), u, preferred_element_type=F32) + run
                    weights.append(jnp.exp2((z - nl) - tail).astype(BF16))
                    vals.append(v_heads[hh][j * blk:(j + 1) * blk, :])
                    if j > 0:
                        run = run + jnp.sum(nl, axis=-1, keepdims=True)
        acc = jnp.dot(jnp.concatenate(weights, axis=1), jnp.concatenate(vals, axis=0), preferred_element_type=F32)
        o_ref[r0:r0 + blk, :] = acc.astype(o_ref.dtype)


def sb_attention(qkv, bsz, seq, tri):
    blk = min(ATT_BLOCK, seq)
    n_pairs = SB_HEADS // 2
    est = 8 * seq * LANES * 2 + 6 * blk * seq * 4
    return pl.pallas_call(
        functools.partial(_sb_attn_kernel, blk=blk, n_blocks=seq // blk),
        grid=(bsz, n_pairs),
        in_specs=[
            pl.BlockSpec((None, None, seq, LANES), lambda b, h: (h, b, 0, 0)),
            pl.BlockSpec((None, None, seq, LANES), lambda b, h: (n_pairs + h, b, 0, 0)),
            pl.BlockSpec((None, None, seq, LANES), lambda b, h: (2 * n_pairs + h, b, 0, 0)),
            pl.BlockSpec((blk, blk), lambda b, h: (0, 0)),
        ],
        out_specs=pl.BlockSpec((seq, LANES), lambda b, h: (0, b * n_pairs + h)),
        out_shape=jax.ShapeDtypeStruct((seq, bsz * D_MODEL), BF16),
        compiler_params=_params(("parallel", "parallel"), est),
        name="sb_attention",
    )(qkv, qkv, qkv, tri)


def _proj_residual_kernel(a_ref, x_ref, w_ref, o_ref, *, glu):
    r = jnp.dot(a_ref[...], w_ref[...], preferred_element_type=F32)
    if glu:
        n = o_ref.shape[-1]
        r = r[:, :n] * _sigmoid(r[:, n:])
    o_ref[...] = x_ref[...] + r


def proj_residual(a, x, x_is_bld, rows, ncol, w, glu=False):
    k, n_w = w.shape
    n = n_w // 2 if glu else n_w
    tl = min(ROW_TILE, rows)
    est = 2 * tl * k * 2 + 4 * tl * n * 4 + 2 * k * n_w * 2 + 2 * tl * n_w * 4
    return pl.pallas_call(
        functools.partial(_proj_residual_kernel, glu=glu),
        grid=(rows // tl, ncol),
        in_specs=[
            pl.BlockSpec((tl, k), lambda i, b: (i, b)),
            _x_spec(x_is_bld, tl, n),
            pl.BlockSpec((k, n_w), lambda i, b: (0, 0)),
        ],
        out_specs=pl.BlockSpec((tl, n), lambda i, b: (i, b)),
        out_shape=jax.ShapeDtypeStruct((rows, ncol * n), F32),
        compiler_params=_params(("parallel", "parallel"), est),
        name="proj_residual_glu" if glu else "proj_residual",
    )(a, x, w)


def _ffn_kernel(x_ref, g_ref, wg_ref, wu_ref, wd_ref, o_ref, h_ref):
    j = pl.program_id(2)

    @pl.when(j == 0)
    def _():
        xf = x_ref[...]
        h_ref[...] = _rms(xf, g_ref[...]).astype(BF16)
        o_ref[...] = xf

    h = h_ref[...]
    gv = jnp.dot(h, wg_ref[...], preferred_element_type=F32)
    uv = jnp.dot(h, wu_ref[...], preferred_element_type=F32)
    act = gv * _sigmoid(gv) * uv
    o_ref[...] += jnp.dot(act.astype(BF16), wd_ref[...], preferred_element_type=F32)


def ffn(x, g, w_gate_up, w_down):
    f, d = w_down.shape
    rows, ncol = x.shape[0], x.shape[1] // d
    tf = FF_CHUNK
    n_chunks = f // tf
    tm = min(ROW_TILE, rows)
    est = 4 * tm * d * 4 + tm * d * 2 + 6 * d * tf * 2 + 3 * tm * tf * 4
    return pl.pallas_call(
        _ffn_kernel,
        grid=(rows // tm, ncol, n_chunks),
        in_specs=[
            pl.BlockSpec((tm, d), lambda i, c, j: (i, c)),
            pl.BlockSpec((1, d), lambda i, c, j: (0, 0)),
            pl.BlockSpec((d, tf), lambda i, c, j: (0, j)),
            pl.BlockSpec((d, tf), lambda i, c, j: (0, n_chunks + j)),
            pl.BlockSpec((tf, d), lambda i, c, j: (j, 0)),
        ],
        out_specs=pl.BlockSpec((tm, d), lambda i, c, j: (i, c)),
        out_shape=jax.ShapeDtypeStruct(x.shape, F32),
        scratch_shapes=[pltpu.VMEM((tm, d), BF16)],
        compiler_params=_params(("parallel", "parallel", "arbitrary"), est),
        name="ffn",
    )(x, g, w_gate_up, w_gate_up, w_down)


def _moe_ffn_kernel(x_ref, g_ref, gates_ref, sel_ref, wg_ref, wu_ref, wd_ref, o_ref, h_ref, rank_ref, *, cap, extra):
    e = pl.program_id(2)
    tm, d = x_ref.shape
    lane = lax.broadcasted_iota(jnp.int32, (1, LANES), 1)

    @pl.when(e == 0)
    def _():
        xf = x_ref[...]
        h_ref[...] = _rms(xf, g_ref[...]).astype(BF16)
        o_ref[...] = xf
        r_i = lax.broadcasted_iota(jnp.int32, (tm, tm), 0)
        c_i = lax.broadcasted_iota(jnp.int32, (tm, tm), 1)
        earlier = jnp.where(c_i < r_i, 1.0, 0.0).astype(BF16)
        rank_ref[...] = jnp.dot(earlier, sel_ref[...].astype(BF16), preferred_element_type=F32)

    mine = lane == e
    sel_e = jnp.sum(jnp.where(mine, sel_ref[...], 0.0), axis=-1, keepdims=True)
    rank_e = jnp.sum(jnp.where(mine, rank_ref[...], 0.0), axis=-1, keepdims=True)
    gate_e = jnp.sum(jnp.where(mine, gates_ref[...], 0.0), axis=-1, keepdims=True)
    count = jnp.sum(sel_e).astype(jnp.int32)
    half = d // 2

    def run_slots(first, width):
        slot = lax.broadcasted_iota(jnp.int32, (1, width), 1).astype(F32)
        pick = jnp.where((rank_e - first == slot) & (sel_e > 0.5), 1.0, 0.0).astype(BF16)
        packed = lax.dot_general(pick, h_ref[...], (((0,), (0,)), ((), ())), preferred_element_type=F32)
        packed = packed.astype(BF16)
        gv = jnp.dot(packed, wg_ref[...], preferred_element_type=F32)
        uv = jnp.dot(packed, wu_ref[...], preferred_element_type=F32)
        act = (gv * _sigmoid(gv) * uv).astype(BF16)
        y = jnp.dot(act, wd_ref[...], preferred_element_type=F32).astype(BF16)
        for c0 in (0, half):
            spread = jnp.dot(pick, y[:, c0:c0 + half], preferred_element_type=F32)
            o_ref[:, c0:c0 + half] += gate_e * spread

    @pl.when(count > 0)
    def _():
        run_slots(jnp.float32(0.0), cap)

    def extra_pass(p, carry):
        run_slots((cap + p * extra).astype(F32), extra)
        return carry

    n_extra = (jnp.maximum(count - cap, 0) + (extra - 1)) // extra
    lax.fori_loop(0, n_extra, extra_pass, 0)


def moe_ffn(x, g, w_gate_up, w_down, gates, sel):
    n_e, f, d = w_down.shape
    rows, ncol = x.shape[0], x.shape[1] // d
    tm = min(MOE_ROW_TILE, rows)
    cap = min(MOE_CAPACITY, tm)
    extra = min(MOE_EXTRA, tm)
    est = (4 * tm * d * 4 + tm * d * 2 + 6 * d * f * 2 + 5 * tm * LANES * 4
           + 2 * tm * tm * 2 + 3 * cap * f * 4 + 2 * cap * d * 4 + 2 * tm * d * 4)
    return pl.pallas_call(
        functools.partial(_moe_ffn_kernel, cap=cap, extra=extra),
        grid=(rows // tm, ncol, n_e),
        in_specs=[
            pl.BlockSpec((tm, d), lambda i, c, e: (i, c)),
            pl.BlockSpec((1, d), lambda i, c, e: (0, 0)),
            pl.BlockSpec((tm, LANES), lambda i, c, e: (i, c)),
            pl.BlockSpec((tm, LANES), lambda i, c, e: (i, c)),
            pl.BlockSpec((None, d, f), lambda i, c, e: (e, 0, 0)),
            pl.BlockSpec((None, d, f), lambda i, c, e: (e, 0, 1)),
            pl.BlockSpec((None, f, d), lambda i, c, e: (e, 0, 0)),
        ],
        out_specs=pl.BlockSpec((tm, d), lambda i, c, e: (i, c)),
        out_shape=jax.ShapeDtypeStruct(x.shape, F32),
        scratch_shapes=[pltpu.VMEM((tm, d), BF16), pltpu.VMEM((tm, LANES), F32)],
        compiler_params=_params(("parallel", "parallel", "arbitrary"), est),
        name="moe_ffn",
    )(x, g, gates, sel, w_gate_up, w_gate_up, w_down)


def _router_kernel(x_ref, g_ref, w_ref, b_ref, o_ref, sel_ref):
    h = _rms(x_ref[...], g_ref[...])
    logits = jnp.dot(h, w_ref[...], preferred_element_type=F32, precision=lax.Precision.HIGHEST) + b_ref[...]
    lane = lax.broadcasted_iota(jnp.int32, logits.shape, 1).astype(F32)
    neg = jnp.float32(-jnp.inf)
    logits = jnp.where(lane < N_EXPERTS, logits, neg)
    m1 = jnp.max(logits, axis=-1, keepdims=True)
    i1 = jnp.min(jnp.where(logits == m1, lane, float(LANES)), axis=-1, keepdims=True)
    rest = jnp.where(lane == i1, neg, logits)
    m2 = jnp.max(rest, axis=-1, keepdims=True)
    i2 = jnp.min(jnp.where(rest == m2, lane, float(LANES)), axis=-1, keepdims=True)
    e2 = jnp.exp(m2 - m1)
    w1 = 1.0 / (1.0 + e2)
    w2 = e2 * w1
    o_ref[...] = jnp.where(lane == i1, w1, jnp.where(lane == i2, w2, 0.0))
    sel_ref[...] = jnp.where((lane == i1) | (lane == i2), 1.0, 0.0)


def router(x, g, w_pad, b_pad):
    d = w_pad.shape[0]
    rows, ncol = x.shape[0], x.shape[1] // d
    tm = min(ROW_TILE, rows)
    est = 2 * tm * d * 4 + 2 * d * LANES * 4 + 4 * tm * LANES * 4 + 2 * tm * d * 4
    return pl.pallas_call(
        _router_kernel,
        grid=(rows // tm, ncol),
        in_specs=[
            pl.BlockSpec((tm, d), lambda i, c: (i, c)),
            pl.BlockSpec((1, d), lambda i, c: (0, 0)),
            pl.BlockSpec((d, LANES), lambda i, c: (0, 0)),
            pl.BlockSpec((1, LANES), lambda i, c: (0, 0)),
        ],
        out_specs=(pl.BlockSpec((tm, LANES), lambda i, c: (i, c)), pl.BlockSpec((tm, LANES), lambda i, c: (i, c))),
        out_shape=(jax.ShapeDtypeStruct((rows, ncol * LANES), F32), jax.ShapeDtypeStruct((rows, ncol * LANES), F32)),
        compiler_params=_params(("parallel", "parallel"), est),
        name="router",
    )(x, g, w_pad, b_pad)


def _s5_discretise_kernel(lre_ref, lim_ref, ldt_ref, bre_ref, bim_ref, are_ref, aim_ref, bbre_ref, bbim_ref):
    lam_re = lre_ref[...]
    lam_im = lim_ref[...]
    dt = jnp.exp(ldt_ref[...])
    mag = jnp.exp(lam_re * dt)
    ang = lam_im * dt
    a_re = mag * jnp.cos(ang)
    a_im = mag * jnp.sin(ang)
    den = lam_re * lam_re + lam_im * lam_im
    num_re = a_re - 1.0
    f_re = (num_re * lam_re + a_im * lam_im) / den
    f_im = (a_im * lam_re - num_re * lam_im) / den
    are_ref[...] = a_re
    aim_ref[...] = a_im
    n_slabs, _, width = bre_ref.shape
    for s in range(n_slabs):
        fr = f_re[:, s * width:(s + 1) * width]
        fi = f_im[:, s * width:(s + 1) * width]
        br = bre_ref[s]
        bi = bim_ref[s]
        bbre_ref[s] = (fr * br - fi * bi).astype(BF16)
        bbim_ref[s] = (fr * bi + fi * br).astype(BF16)


def s5_discretise(lam_re_row, lam_im_row, log_dt_row, b_re_blk, b_im_blk):
    n_state = lam_re_row.shape[1]
    shp = b_re_blk.shape
    return pl.pallas_call(
        _s5_discretise_kernel,
        out_shape=(
            jax.ShapeDtypeStruct((1, n_state), F32),
            jax.ShapeDtypeStruct((1, n_state), F32),
            jax.ShapeDtypeStruct(shp, BF16),
            jax.ShapeDtypeStruct(shp, BF16),
        ),
        name="s5_discretise",
    )(lam_re_row, lam_im_row, log_dt_row, b_re_blk, b_im_blk)


def _gelu_tanh(y):
    c = math.sqrt(2.0 / math.pi)
    return 0.5 * y * (1.0 + jnp.tanh(c * (y + 0.044715 * (y * y * y))))


def _s5_kernel(x_ref, g_ref, are_ref, aim_ref, bbre_ref, bbim_ref, cre_ref, cim_ref, d_ref, o_ref,
               sre_ref, sim_ref, stre_ref, stim_ref, h_ref, *, bsz, steps):
    i = pl.program_id(0)

    @pl.when(i == 0)
    def _():
        stre_ref[...] = jnp.zeros_like(stre_ref)
        stim_ref[...] = jnp.zeros_like(stim_ref)

    h_ref[...] = _rms(x_ref[...], g_ref[...])
    n_slabs, slab, width = bbre_ref.shape
    for s in range(n_slabs):
        hs = h_ref[:, s * slab:(s + 1) * slab].astype(BF16)
        sre_ref[:, s * width:(s + 1) * width] = jnp.dot(hs, bbre_ref[s], preferred_element_type=F32)
        sim_ref[:, s * width:(s + 1) * width] = jnp.dot(hs, bbim_ref[s], preferred_element_type=F32)

    n_state = sre_ref.shape[1]
    cw = S5_LANE_CHUNK
    for c in range(n_state // cw):
        lanes = slice(c * cw, (c + 1) * cw)
        ar = jnp.broadcast_to(are_ref[:, lanes], (bsz, cw))
        ai = jnp.broadcast_to(aim_ref[:, lanes], (bsz, cw))

        def step(t, carry):
            xr, xi = carry
            r0 = pl.multiple_of(t * bsz, bsz)
            nr = ar * xr - ai * xi + sre_ref[pl.ds(r0, bsz), lanes]
            ni = ar * xi + ai * xr + sim_ref[pl.ds(r0, bsz), lanes]
            sre_ref[pl.ds(r0, bsz), lanes] = nr
            sim_ref[pl.ds(r0, bsz), lanes] = ni
            return nr, ni

        xr, xi = lax.fori_loop(0, steps, step, (stre_ref[:, lanes], stim_ref[:, lanes]), unroll=8)
        stre_ref[:, lanes] = xr
        stim_ref[:, lanes] = xi

    dsk = d_ref[...]
    for s in range(n_slabs):
        xr = sre_ref[:, s * width:(s + 1) * width].astype(BF16)
        xi = sim_ref[:, s * width:(s + 1) * width].astype(BF16)
        y = (jnp.dot(xr, cre_ref[s], preferred_element_type=F32)
             - jnp.dot(xi, cim_ref[s], preferred_element_type=F32))
        cols = slice(s * slab, (s + 1) * slab)
        y = y + dsk[:, cols] * h_ref[:, cols]
        o_ref[:, cols] = _gelu_tanh(y).astype(o_ref.dtype)


def s5_mixer(x, g, a_re, a_im, bb_re, bb_im, c_re_blk, c_im_blk, d_skip, bsz):
    t, d = x.shape
    seq = t // bsz
    steps = min(S5_STEPS, seq)
    rows = steps * bsz
    n_state = a_re.shape[1]
    wbytes = 2 * (bb_re.size + c_re_blk.size) * 2
    est = 2 * rows * d * 4 + 2 * rows * d * 2 + 2 * wbytes + 2 * rows * n_state * 4 + 4 * rows * 1024 * 4
    const3 = lambda i: (0, 0, 0)
    const2 = lambda i: (0, 0)
    return pl.pallas_call(
        functools.partial(_s5_kernel, bsz=bsz, steps=steps),
        grid=(seq // steps,),
        in_specs=[
            pl.BlockSpec((rows, d), lambda i: (i, 0)),
            pl.BlockSpec((1, d), const2),
            pl.BlockSpec((1, n_state), const2),
            pl.BlockSpec((1, n_state), const2),
            pl.BlockSpec(bb_re.shape, const3),
            pl.BlockSpec(bb_im.shape, const3),
            pl.BlockSpec(c_re_blk.shape, const3),
            pl.BlockSpec(c_im_blk.shape, const3),
            pl.BlockSpec((1, d), const2),
        ],
        out_specs=pl.BlockSpec((rows, d), lambda i: (i, 0)),
        out_shape=jax.ShapeDtypeStruct((t, d), BF16),
        scratch_shapes=[
            pltpu.VMEM((rows, n_state), F32),
            pltpu.VMEM((rows, n_state), F32),
            pltpu.VMEM((bsz, n_state), F32),
            pltpu.VMEM((bsz, n_state), F32),
            pltpu.VMEM((rows, d), F32),
        ],
        compiler_params=_params(("arbitrary",), est),
        name="s5_mixer",
    )(x, g, a_re, a_im, bb_re, bb_im, c_re_blk, c_im_blk, d_skip)


def _mla_proj_kernel(x_ref, g_ref, win_ref, qn_ref, kvn_ref, wa_ref, wb_ref, wk_ref, wv_ref, cos_ref, sin_ref,
                     q_ref, k_ref, v_ref, *, scale):
    h = _rms(x_ref[...], g_ref[...]).astype(BF16)
    c = jnp.dot(h, win_ref[...], preferred_element_type=F32)
    qc = _rms(c[:, :MLA_Q_RANK], qn_ref[...]).astype(BF16)
    kvc = _rms(c[:, MLA_Q_RANK:MLA_Q_RANK + MLA_KV_RANK], kvn_ref[...]).astype(BF16)
    cos = cos_ref[...]
    sin = sin_ref[...]
    base = MLA_Q_RANK + MLA_KV_RANK
    k_rope = c[:, base:base + LANES] * cos + c[:, base + LANES:base + 2 * LANES] * sin
    lane = lax.broadcasted_iota(jnp.int32, (1, LANES), 1)
    q_cos = (cos + jnp.where(lane >= 2 * MLA_ROPE, 1.0, 0.0)) * scale
    q_sin = sin * scale
    k_in = jnp.concatenate([kvc, k_rope.astype(BF16)], axis=-1)
    per = 4
    for cblk in range(MLA_HEADS // per):
        cols = slice(cblk * per * LANES, (cblk + 1) * per * LANES)
        qa = jnp.dot(qc, wa_ref[:, cols], preferred_element_type=F32)
        qb = jnp.dot(qc, wb_ref[:, cols], preferred_element_type=F32)
        kk = jnp.dot(k_in, wk_ref[:, cols], preferred_element_type=F32)
        for jj in range(per):
            sl = slice(jj * LANES, (jj + 1) * LANES)
            q_ref[cblk * per + jj] = (qa[:, sl] * q_cos + qb[:, sl] * q_sin).astype(BF16)
            k_ref[cblk * per + jj] = kk[:, sl].astype(BF16)
    vv = jnp.dot(kvc, wv_ref[...], preferred_element_type=F32)
    for jj in range(MLA_HEADS // 2):
        v_ref[jj] = vv[:, jj * LANES:(jj + 1) * LANES].astype(BF16)


def mla_proj(x, bsz, seq, g, w_in, q_norm, kv_norm, wa, wb, wk, wv, cos_tab, sin_tab):
    d = D_MODEL
    tl = min(ROW_TILE, seq)
    scale = float((MLA_NOPE + MLA_ROPE) ** -0.5) * LOG2_E
    wbytes = (w_in.size + wa.size + wb.size + wk.size + wv.size) * 2
    est = 2 * tl * d * 4 + 2 * wbytes + 2 * 5 * tl * 1024 * 2 + 8 * tl * 1024 * 4
    c2 = lambda i, b: (0, 0)
    nh = MLA_HEADS
    return pl.pallas_call(
        functools.partial(_mla_proj_kernel, scale=scale),
        grid=(seq // tl, bsz),
        in_specs=[
            _x_spec(False, tl, d),
            pl.BlockSpec((1, d), c2),
            pl.BlockSpec(w_in.shape, c2),
            pl.BlockSpec((1, MLA_Q_RANK), c2),
            pl.BlockSpec((1, MLA_KV_RANK), c2),
            pl.BlockSpec(wa.shape, c2),
            pl.BlockSpec(wb.shape, c2),
            pl.BlockSpec(wk.shape, c2),
            pl.BlockSpec(wv.shape, c2),
            pl.BlockSpec((tl, LANES), lambda i, b: (i, 0)),
            pl.BlockSpec((tl, LANES), lambda i, b: (i, 0)),
        ],
        out_specs=(
            pl.BlockSpec((None, nh, tl, LANES), lambda i, b: (b, 0, i, 0)),
            pl.BlockSpec((None, nh, tl, LANES), lambda i, b: (b, 0, i, 0)),
            pl.BlockSpec((None, nh // 2, tl, LANES), lambda i, b: (b, 0, i, 0)),
        ),
        out_shape=(
            jax.ShapeDtypeStruct((bsz, nh, seq, LANES), BF16),
            jax.ShapeDtypeStruct((bsz, nh, seq, LANES), BF16),
            jax.ShapeDtypeStruct((bsz, nh // 2, seq, LANES), BF16),
        ),
        compiler_params=_params(("parallel", "parallel"), est),
        name="mla_proj",
    )(x, g, w_in, q_norm, kv_norm, wa, wb, wk, wv, cos_tab, sin_tab)


def _mla_attn_kernel(q_ref, k_ref, v_ref, o_ref, *, blk, n_blocks):
    lane = lax.broadcasted_iota(jnp.int32, (1, LANES), 1)
    row = lax.broadcasted_iota(jnp.int32, (blk, blk), 0)
    col = lax.broadcasted_iota(jnp.int32, (blk, blk), 1)
    causal = col <= row
    nt = (((1,), (1,)), ((), ()))
    v_pair = v_ref[...]
    v_heads = [jnp.where(lane < MLA_V, v_pair, jnp.zeros_like(v_pair)),
               jnp.where(lane >= MLA_V, v_pair, jnp.zeros_like(v_pair))]
    for qi in range(n_blocks):
        r0 = qi * blk
        probs, vals, inv_l = [], [], []
        for hh in range(2):
            qh = q_ref[hh, r0:r0 + blk, :]
            s_d = lax.dot_general(qh, k_ref[hh, r0:r0 + blk, :], nt, preferred_element_type=F32)
            s_d = jnp.where(causal, s_d, -jnp.inf)
            m = jnp.max(s_d, axis=-1, keepdims=True)
            if qi > 0:
                s_o = lax.dot_general(qh, k_ref[hh, 0:r0, :], nt, preferred_element_type=F32)
                m = jnp.maximum(m, jnp.max(s_o, axis=-1, keepdims=True))
                p_o = jnp.exp2(s_o - m)
                probs.append(p_o.astype(BF16))
            p_d = jnp.exp2(s_d - m)
            probs.append(p_d.astype(BF16))
            l = jnp.sum(p_d, axis=-1, keepdims=True)
            if qi > 0:
                l = l + jnp.sum(p_o, axis=-1, keepdims=True)
            inv_l.append(1.0 / l)
            vals.append(v_heads[hh][0:r0 + blk, :])
        acc = jnp.dot(jnp.concatenate(probs, axis=1), jnp.concatenate(vals, axis=0), preferred_element_type=F32)
        o_ref[r0:r0 + blk, :] = (acc * jnp.where(lane < MLA_V, inv_l[0], inv_l[1])).astype(o_ref.dtype)


def mla_attention(q, k, v, bsz, seq):
    blk = min(ATT_BLOCK, seq)
    n_pairs = MLA_HEADS // 2
    est = 12 * seq * LANES * 2 + 6 * blk * seq * 4
    return pl.pallas_call(
        functools.partial(_mla_attn_kernel, blk=blk, n_blocks=seq // blk),
        grid=(bsz, n_pairs),
        in_specs=[
            pl.BlockSpec((None, 2, seq, LANES), lambda b, h: (b, h, 0, 0)),
            pl.BlockSpec((None, 2, seq, LANES), lambda b, h: (b, h, 0, 0)),
            pl.BlockSpec((None, None, seq, LANES), lambda b, h: (b, h, 0, 0)),
        ],
        out_specs=pl.BlockSpec((seq, LANES), lambda b, h: (0, b * n_pairs + h)),
        out_shape=jax.ShapeDtypeStruct((seq, bsz * D_MODEL), BF16),
        compiler_params=_params(("parallel", "parallel"), est),
        name="mla_attention",
    )(q, k, v)


def _final_norm_kernel(x_ref, g_ref, o_ref):
    o_ref[...] = _rms(x_ref[...], g_ref[...])


def final_norm(x, g, bsz, seq):
    d = D_MODEL
    tl = min(ROW_TILE, seq)
    return pl.pallas_call(
        _final_norm_kernel,
        grid=(seq // tl, bsz),
        in_specs=[_x_spec(False, tl, d), pl.BlockSpec((1, d), lambda i, b: (0, 0))],
        out_specs=pl.BlockSpec((None, tl, d), lambda i, b: (b, i, 0)),
        out_shape=jax.ShapeDtypeStruct((bsz, seq, d), F32),
        compiler_params=_params(("parallel", "parallel"), 8 * tl * d * 4),
        name="final_norm",
    )(x, g)


def _row(v):
    return v.reshape(1, -1).astype(F32)


def _sb_qkv_weight(w_qkv):
    scale = SB_HEAD_DIM ** -0.5 * LOG2_E
    col_scale = jnp.concatenate([jnp.full((D_MODEL,), scale, F32), jnp.ones((2 * D_MODEL,), F32)])
    return (w_qkv * col_scale[None, :]).astype(BF16)


def _s5_block_diag(b_re, b_im, c_re, c_im):
    per = S5_SLAB // S5_GROUP
    n_slabs = S5_GROUPS // per
    eye = jnp.eye(per, dtype=F32)

    def b_blk(b):
        b4 = b.reshape(n_slabs, per, S5_STATE, S5_GROUP)
        return jnp.einsum('sgpc,gh->sgchp', b4, eye).reshape(n_slabs, per * S5_GROUP, per * S5_STATE)

    def c_blk(c):
        c4 = c.reshape(n_slabs, per, S5_GROUP, S5_STATE)
        return jnp.einsum('sgcp,gh->sgphc', c4, eye).reshape(n_slabs, per * S5_STATE, per * S5_GROUP)

    return b_blk(b_re), b_blk(b_im), c_blk(c_re).astype(BF16), c_blk(c_im).astype(BF16)


def _rotate_half_cols(w):
    half = w.shape[-1] // 2
    return jnp.concatenate([-w[..., half:], w[..., :half]], axis=-1)


def _mla_weights(w_in, w_q_b, w_kv_b):
    base = MLA_Q_RANK + MLA_KV_RANK
    w_kr = w_in[:, base:]
    pad = jnp.zeros((D_MODEL, LANES - MLA_ROPE), F32)
    w_in2 = jnp.concatenate([w_in[:, :base], w_kr, pad, _rotate_half_cols(w_kr), pad], axis=-1).astype(BF16)
    wq = w_q_b.reshape(MLA_Q_RANK, MLA_HEADS, MLA_NOPE + MLA_ROPE)
    q_nope, q_rope = wq[..., :MLA_NOPE], wq[..., MLA_NOPE:]
    zq = jnp.zeros((MLA_Q_RANK, MLA_HEADS, MLA_ROPE), F32)
    wa = jnp.concatenate([q_rope, zq, q_nope], axis=-1).reshape(MLA_Q_RANK, MLA_HEADS * LANES).astype(BF16)
    wb = jnp.concatenate([_rotate_half_cols(q_rope), zq, jnp.zeros_like(q_nope)], axis=-1)
    wb = wb.reshape(MLA_Q_RANK, MLA_HEADS * LANES).astype(BF16)
    wkv = w_kv_b.reshape(MLA_KV_RANK, MLA_HEADS, MLA_NOPE + MLA_V)
    k_nope, v_w = wkv[..., :MLA_NOPE], wkv[..., MLA_NOPE:]
    zk = jnp.zeros((MLA_KV_RANK, MLA_HEADS, 2 * MLA_ROPE), F32)
    wk_top = jnp.concatenate([zk, k_nope], axis=-1).reshape(MLA_KV_RANK, MLA_HEADS * LANES)
    route = jnp.concatenate([jnp.eye(LANES, MLA_ROPE, dtype=F32), jnp.zeros((LANES, LANES - MLA_ROPE), F32)], axis=-1)
    wk_bot = jnp.tile(route, (1, MLA_HEADS))
    wk = jnp.concatenate([wk_top, wk_bot], axis=0).astype(BF16)
    wv = v_w.reshape(MLA_KV_RANK, MLA_HEADS * MLA_V).astype(BF16)
    return w_in2, wa, wb, wk, wv


def _rope_tables(seq, bsz):
    pos = jnp.arange(seq, dtype=F32)
    inv_freq = ROPE_THETA ** (-jnp.arange(0, MLA_ROPE, 2, dtype=F32) / MLA_ROPE)
    ang = pos[:, None] * inv_freq[None, :]
    zeros = jnp.zeros((seq, LANES - MLA_ROPE), F32)
    cos = jnp.concatenate([jnp.cos(ang), jnp.cos(ang), zeros], axis=-1)
    sin = jnp.concatenate([jnp.sin(ang), jnp.sin(ang), zeros], axis=-1)
    return cos, sin


def moe_layer(flat, g_ffn, w_router, b_router, w_gate_up, w_down):
    w_r = jnp.pad(w_router, ((0, 0), (0, LANES - N_EXPERTS)))
    b_r = jnp.pad(b_router, (0, LANES - N_EXPERTS)).reshape(1, LANES)
    gates, sel = router(flat, g_ffn, w_r, b_r)
    return moe_ffn(flat, g_ffn, w_gate_up.astype(BF16), w_down.astype(BF16), gates, sel)


def kernel(x, norm_mix, norm_ffn, final_norm_g, sb_w_qkv, sb_w_o, s5_lambda_re, s5_lambda_im, s5_log_dt, s5_b_re,
           s5_b_im, s5_c_re, s5_c_im, s5_d, s5_w_glu, mla_w_in, mla_q_norm, mla_w_q_b, mla_kv_norm, mla_w_kv_b,
           mla_w_o, ffn_w_gate_up, ffn_w_down, moe_w_router, moe_b_router, moe_w_gate_up, moe_w_down):
    bsz, seq, d = x.shape
    t = bsz * seq
    blk = min(ATT_BLOCK, seq)
    tri = (jnp.arange(blk)[:, None] > jnp.arange(blk)[None, :]).astype(BF16)
    cos_tab, sin_tab = _rope_tables(seq, bsz)

    cur = x
    is_bld = True

    def wide(a):
        return a if a.shape[0] == seq else a.reshape(seq, bsz * d)

    for i in range(DEPTH):
        kind, slot = i % N_MIXERS, i // N_MIXERS
        g_mix = _row(norm_mix[i])
        if kind == 0:
            cur = cur if is_bld else wide(cur)
            qkv = norm_proj_blocks(cur, is_bld, bsz, seq, g_mix, _sb_qkv_weight(sb_w_qkv[slot]))
            o = sb_attention(qkv, bsz, seq, tri)
            cur = proj_residual(o, cur, is_bld, seq, bsz, sb_w_o[slot].astype(BF16))
        elif kind == 1:
            b_re_blk, b_im_blk, c_re_blk, c_im_blk = _s5_block_diag(
                s5_b_re[slot], s5_b_im[slot], s5_c_re[slot], s5_c_im[slot])
            log_dt_row = jnp.repeat(s5_log_dt[slot], S5_STATE).reshape(1, -1)
            a_re, a_im, bb_re, bb_im = s5_discretise(
                _row(s5_lambda_re[slot]), _row(s5_lambda_im[slot]), log_dt_row, b_re_blk, b_im_blk)
            if is_bld:
                cur = jnp.transpose(cur, (1, 0, 2))
            flat = cur.reshape(t, d)
            y = s5_mixer(flat, g_mix, a_re, a_im, bb_re, bb_im, c_re_blk, c_im_blk, _row(s5_d[slot]), bsz)
            cur = proj_residual(y, flat, False, t, 1, s5_w_glu[slot].astype(BF16), glu=True)
        else:
            if is_bld:
                cur = jnp.transpose(cur, (1, 0, 2))
            cur = wide(cur)
            w_in2, wa, wb, wk, wv = _mla_weights(mla_w_in[slot], mla_w_q_b[slot], mla_w_kv_b[slot])
            q, k, v = mla_proj(cur, bsz, seq, g_mix, w_in2, _row(mla_q_norm[slot]), _row(mla_kv_norm[slot]),
                               wa, wb, wk, wv, cos_tab, sin_tab)
            o = mla_attention(q, k, v, bsz, seq)
            cur = proj_residual(o, cur, False, seq, bsz, mla_w_o[slot].astype(BF16))
        is_bld = False
        g_ffn = _row(norm_ffn[i])
        if i % 2 == 0:
            cur = ffn(cur, g_ffn, ffn_w_gate_up[i // 2].astype(BF16), ffn_w_down[i // 2].astype(BF16))
        else:
            cur = moe_layer(cur, g_ffn, moe_w_router[i // 2], moe_b_router[i // 2], moe_w_gate_up[i // 2],
                            moe_w_down[i // 2])
    return final_norm(wide(cur), _row(final_norm_g), bsz, seq)
```

```python
import functools
import math

import jax
import jax.numpy as jnp
from jax import lax
from jax.experimental import pallas as pl
from jax.experimental.pallas import tpu as pltpu

F32 = jnp.float32
BF16 = jnp.bfloat16

D_MODEL = 1024
DEPTH = 4
N_MIXERS = 3
RMS_EPS = 1e-6
LOG2_E = 1.0 / math.log(2.0)
SB_HEADS = 16
SB_HEAD_DIM = D_MODEL // SB_HEADS
S5_GROUP = 16
S5_GROUPS = D_MODEL // S5_GROUP
S5_STATE = 64
MLA_HEADS = 16
MLA_NOPE = 64
MLA_ROPE = 32
MLA_V = 64
MLA_Q_RANK = 384
MLA_KV_RANK = 256
ROPE_THETA = 10000.0
D_FF = 2816
N_EXPERTS = 8
TOP_K = 2
D_FF_EXPERT = 1408

LANES = 128
SUBLANES = 8
MXU_DIM = 256
VMEM_BYTES_V7X = 64 * 1024 * 1024
VMEM_CAP = VMEM_BYTES_V7X - 8 * 1024 * 1024

ROW_TILE = 512
ATT_BLOCK = MXU_DIM
S5_STEPS = 64
S5_SLAB = MXU_DIM
S5_LANE_CHUNK = 512
FF_CHUNK = 1408
MOE_ROW_TILE = 1024
MOE_CAPACITY = 256
MOE_EXTRA = 128
CAST_BLOCK_BYTES = 4 * 1024 * 1024


def _params(semantics, est_bytes):
    limit = int(min(VMEM_CAP, max(32 * 1024 * 1024, est_bytes * 5 // 4)))
    return pltpu.CompilerParams(dimension_semantics=semantics, vmem_limit_bytes=limit)


def _rms(xf, g):
    ms = jnp.mean(xf * xf, axis=-1, keepdims=True)
    return xf * lax.rsqrt(ms + RMS_EPS) * g


def _sigmoid(v):
    return 1.0 / (1.0 + jnp.exp(-v))


def _x_spec(x_is_bld, tl, d):
    if x_is_bld:
        return pl.BlockSpec((None, tl, d), lambda i, b: (b, i, 0))
    return pl.BlockSpec((tl, d), lambda i, b: (i, b))


def _norm_proj_blocks_kernel(x_ref, g_ref, w_ref, o_ref):
    h = _rms(x_ref[...], g_ref[...]).astype(BF16)
    nb = o_ref.shape[0]
    per = 4
    for c in range(nb // per):
        r = jnp.dot(h, w_ref[:, c * per * LANES:(c + 1) * per * LANES], preferred_element_type=F32)
        for jj in range(per):
            o_ref[c * per + jj] = r[:, jj * LANES:(jj + 1) * LANES].astype(BF16)


def norm_proj_blocks(x, x_is_bld, bsz, seq, g, w):
    d, n = w.shape
    nb = n // LANES
    tl = min(ROW_TILE, seq)
    est = 2 * tl * d * 4 + 2 * d * n * 2 + 2 * n * tl * 2 + 4 * tl * 512 * 4
    return pl.pallas_call(
        _norm_proj_blocks_kernel,
        grid=(seq // tl, bsz),
        in_specs=[
            _x_spec(x_is_bld, tl, d),
            pl.BlockSpec((1, d), lambda i, b: (0, 0)),
            pl.BlockSpec((d, n), lambda i, b: (0, 0)),
        ],
        out_specs=pl.BlockSpec((nb, None, tl, LANES), lambda i, b: (0, b, i, 0)),
        out_shape=jax.ShapeDtypeStruct((nb, bsz, seq, LANES), BF16),
        compiler_params=_params(("parallel", "parallel"), est),
        name="norm_proj_blocks",
    )(x, g, w)


def _sb_attn_kernel(q_ref, k_ref, v_ref, u_ref, o_ref, *, blk, n_blocks):
    lane = lax.broadcasted_iota(jnp.int32, (1, LANES), 1)
    row = lax.broadcasted_iota(jnp.int32, (blk, blk), 0)
    col = lax.broadcasted_iota(jnp.int32, (blk, blk), 1)
    strict = col < row
    u = u_ref[...]
    nt = (((1,), (1,)), ((), ()))

    def neg_log2_one_minus_beta(z):
        return jnp.maximum(z, 0.0) + jnp.log2(1.0 + jnp.exp2(-jnp.abs(z)))

    v_pair = v_ref[...]
    v_heads = [jnp.where(lane < SB_HEAD_DIM, v_pair, jnp.zeros_like(v_pair)),
               jnp.where(lane >= SB_HEAD_DIM, v_pair, jnp.zeros_like(v_pair))]
    for qi in range(n_blocks):
        r0 = qi * blk
        q = q_ref[r0:r0 + blk, :]
        weights, vals = [], []
        for hh in range(2):
            head_lanes = (lane >= SB_HEAD_DIM) if hh else (lane < SB_HEAD_DIM)
            qh = jnp.where(head_lanes, q, jnp.zeros_like(q))
            z = lax.dot_general(qh, k_ref[r0:r0 + blk, :], nt, preferred_element_type=F32)
            nl = jnp.where(strict, neg_log2_one_minus_beta(z), 0.0)
            tail = jnp.dot(nl.astype(BF16), u, preferred_element_type=F32)
            w = jnp.where(strict, jnp.exp2((z - nl) - tail), 0.0)
            weights.append(w.astype(BF16))
            vals.append(v_heads[hh][r0:r0 + blk, :])
            run = jnp.sum(nl, axis=-1, keepdims=True)
            if qi > 0:
                z_off = lax.dot_general(qh, k_ref[0:r0, :], nt, preferred_element_type=F32)
                for j in range(qi - 1, -1, -1):
                    z = z_off[:, j * blk:(j + 1) * blk]
                    nl = neg_log2_one_minus_beta(z)
                    tail = jnp.dot(nl.astype(BF16), u, preferred_element_type=F32) + run
                    weights.append(jnp.exp2((z - nl) - tail).astype(BF16))
                    vals.append(v_heads[hh][j * blk:(j + 1) * blk, :])
                    if j > 0:
                        run = run + jnp.sum(nl, axis=-1, keepdims=True)
        acc = jnp.dot(jnp.concatenate(weights, axis=1), jnp.concatenate(vals, axis=0), preferred_element_type=F32)
        o_ref[r0:r0 + blk, :] = acc.astype(o_ref.dtype)


def sb_attention(qkv, bsz, seq, tri):
    blk = min(ATT_BLOCK, seq)
    n_pairs = SB_HEADS // 2
    est = 8 * seq * LANES * 2 + 6 * blk * seq * 4
    return pl.pallas_call(
        functools.partial(_sb_attn_kernel, blk=blk, n_blocks=seq // blk),
        grid=(bsz, n_pairs),
        in_specs=[
            pl.BlockSpec((None, None, seq, LANES), lambda b, h: (h, b, 0, 0)),
            pl.BlockSpec((None, None, seq, LANES), lambda b, h: (n_pairs + h, b, 0, 0)),
            pl.BlockSpec((None, None, seq, LANES), lambda b, h: (2 * n_pairs + h, b, 0, 0)),
            pl.BlockSpec((blk, blk), lambda b, h: (0, 0)),
        ],
        out_specs=pl.BlockSpec((seq, LANES), lambda b, h: (0, b * n_pairs + h)),
        out_shape=jax.ShapeDtypeStruct((seq, bsz * D_MODEL), BF16),
        compiler_params=_params(("parallel", "parallel"), est),
        name="sb_attention",
    )(qkv, qkv, qkv, tri)


def _proj_residual_kernel(a_ref, x_ref, w_ref, o_ref, *, glu):
    r = jnp.dot(a_ref[...], w_ref[...], preferred_element_type=F32)
    if glu:
        n = o_ref.shape[-1]
        r = r[:, :n] * _sigmoid(r[:, n:])
    o_ref[...] = x_ref[...] + r


def proj_residual(a, x, x_is_bld, rows, ncol, w, glu=False):
    k, n_w = w.shape
    n = n_w // 2 if glu else n_w
    tl = min(ROW_TILE, rows)
    est = 2 * tl * k * 2 + 4 * tl * n * 4 + 2 * k * n_w * 2 + 2 * tl * n_w * 4
    return pl.pallas_call(
        functools.partial(_proj_residual_kernel, glu=glu),
        grid=(rows // tl, ncol),
        in_specs=[
            pl.BlockSpec((tl, k), lambda i, b: (i, b)),
            _x_spec(x_is_bld, tl, n),
            pl.BlockSpec((k, n_w), lambda i, b: (0, 0)),
        ],
        out_specs=pl.BlockSpec((tl, n), lambda i, b: (i, b)),
        out_shape=jax.ShapeDtypeStruct((rows, ncol * n), F32),
        compiler_params=_params(("parallel", "parallel"), est),
        name="proj_residual_glu" if glu else "proj_residual",
    )(a, x, w)


def _ffn_kernel(x_ref, g_ref, wg_ref, wu_ref, wd_ref, o_ref, h_ref):
    j = pl.program_id(2)

    @pl.when(j == 0)
    def _():
        xf = x_ref[...]
        h_ref[...] = _rms(xf, g_ref[...]).astype(BF16)
        o_ref[...] = xf

    h = h_ref[...]
    gv = jnp.dot(h, wg_ref[...], preferred_element_type=F32)
    uv = jnp.dot(h, wu_ref[...], preferred_element_type=F32)
    act = gv * _sigmoid(gv) * uv
    o_ref[...] += jnp.dot(act.astype(BF16), wd_ref[...], preferred_element_type=F32)


def ffn(x, g, w_gate_up, w_down):
    f, d = w_down.shape
    rows, ncol = x.shape[0], x.shape[1] // d
    tf = FF_CHUNK
    n_chunks = f // tf
    tm = min(ROW_TILE, rows)
    est = 4 * tm * d * 4 + tm * d * 2 + 6 * d * tf * 2 + 3 * tm * tf * 4
    return pl.pallas_call(
        _ffn_kernel,
        grid=(rows // tm, ncol, n_chunks),
        in_specs=[
            pl.BlockSpec((tm, d), lambda i, c, j: (i, c)),
            pl.BlockSpec((1, d), lambda i, c, j: (0, 0)),
            pl.BlockSpec((d, tf), lambda i, c, j: (0, j)),
            pl.BlockSpec((d, tf), lambda i, c, j: (0, n_chunks + j)),
            pl.BlockSpec((tf, d), lambda i, c, j: (j, 0)),
        ],
        out_specs=pl.BlockSpec((tm, d), lambda i, c, j: (i, c)),
        out_shape=jax.ShapeDtypeStruct(x.shape, F32),
        scratch_shapes=[pltpu.VMEM((tm, d), BF16)],
        compiler_params=_params(("parallel", "parallel", "arbitrary"), est),
        name="ffn",
    )(x, g, w_gate_up, w_gate_up, w_down)


def _moe_ffn_kernel(x_ref, g_ref, gates_ref, sel_ref, wg_ref, wu_ref, wd_ref, o_ref, h_ref, rank_ref, *, cap, extra):
    e = pl.program_id(2)
    tm, d = x_ref.shape
    lane = lax.broadcasted_iota(jnp.int32, (1, LANES), 1)

    @pl.when(e == 0)
    def _():
        xf = x_ref[...]
        h_ref[...] = _rms(xf, g_ref[...]).astype(BF16)
        o_ref[...] = xf
        r_i = lax.broadcasted_iota(jnp.int32, (tm, tm), 0)
        c_i = lax.broadcasted_iota(jnp.int32, (tm, tm), 1)
        earlier = jnp.where(c_i < r_i, 1.0, 0.0).astype(BF16)
        rank_ref[...] = jnp.dot(earlier, sel_ref[...].astype(BF16), preferred_element_type=F32)

    mine = lane == e
    sel_e = jnp.sum(jnp.where(mine, sel_ref[...], 0.0), axis=-1, keepdims=True)
    rank_e = jnp.sum(jnp.where(mine, rank_ref[...], 0.0), axis=-1, keepdims=True)
    gate_e = jnp.sum(jnp.where(mine, gates_ref[...], 0.0), axis=-1, keepdims=True)
    count = jnp.sum(sel_e).astype(jnp.int32)
    half = d // 2

    def run_slots(first, width):
        slot = lax.broadcasted_iota(jnp.int32, (1, width), 1).astype(F32)
        pick = jnp.where((rank_e - first == slot) & (sel_e > 0.5), 1.0, 0.0).astype(BF16)
        packed = lax.dot_general(pick, h_ref[...], (((0,), (0,)), ((), ())), preferred_element_type=F32)
        packed = packed.astype(BF16)
        gv = jnp.dot(packed, wg_ref[...], preferred_element_type=F32)
        uv = jnp.dot(packed, wu_ref[...], preferred_element_type=F32)
        act = (gv * _sigmoid(gv) * uv).astype(BF16)
        y = jnp.dot(act, wd_ref[...], preferred_element_type=F32).astype(BF16)
        for c0 in (0, half):
            spread = jnp.dot(pick, y[:, c0:c0 + half], preferred_element_type=F32)
            o_ref[:, c0:c0 + half] += gate_e * spread

    @pl.when(count > 0)
    def _():
        run_slots(jnp.float32(0.0), cap)

    def extra_pass(p, carry):
        run_slots((cap + p * extra).astype(F32), extra)
        return carry

    n_extra = (jnp.maximum(count - cap, 0) + (extra - 1)) // extra
    lax.fori_loop(0, n_extra, extra_pass, 0)


def moe_ffn(x, g, w_gate_up, w_down, gates, sel):
    n_e, f, d = w_down.shape
    rows, ncol = x.shape[0], x.shape[1] // d
    tm = min(MOE_ROW_TILE, rows)
    cap = min(MOE_CAPACITY, tm)
    extra = min(MOE_EXTRA, tm)
    est = (4 * tm * d * 4 + tm * d * 2 + 6 * d * f * 2 + 5 * tm * LANES * 4
           + 2 * tm * tm * 2 + 3 * cap * f * 4 + 2 * cap * d * 4 + 2 * tm * d * 4)
    return pl.pallas_call(
        functools.partial(_moe_ffn_kernel, cap=cap, extra=extra),
        grid=(rows // tm, ncol, n_e),
        in_specs=[
            pl.BlockSpec((tm, d), lambda i, c, e: (i, c)),
            pl.BlockSpec((1, d), lambda i, c, e: (0, 0)),
            pl.BlockSpec((tm, LANES), lambda i, c, e: (i, c)),
            pl.BlockSpec((tm, LANES), lambda i, c, e: (i, c)),
            pl.BlockSpec((None, d, f), lambda i, c, e: (e, 0, 0)),
            pl.BlockSpec((None, d, f), lambda i, c, e: (e, 0, 1)),
            pl.BlockSpec((None, f, d), lambda i, c, e: (e, 0, 0)),
        ],
        out_specs=pl.BlockSpec((tm, d), lambda i, c, e: (i, c)),
        out_shape=jax.ShapeDtypeStruct(x.shape, F32),
        scratch_shapes=[pltpu.VMEM((tm, d), BF16), pltpu.VMEM((tm, LANES), F32)],
        compiler_params=_params(("parallel", "parallel", "arbitrary"), est),
        name="moe_ffn",
    )(x, g, gates, sel, w_gate_up, w_gate_up, w_down)


def _router_kernel(x_ref, g_ref, w_ref, b_ref, o_ref, sel_ref):
    h = _rms(x_ref[...], g_ref[...])
    logits = jnp.dot(h, w_ref[...], preferred_element_type=F32, precision=lax.Precision.HIGHEST) + b_ref[...]
    lane = lax.broadcasted_iota(jnp.int32, logits.shape, 1).astype(F32)
    neg = jnp.float32(-jnp.inf)
    logits = jnp.where(lane < N_EXPERTS, logits, neg)
    m1 = jnp.max(logits, axis=-1, keepdims=True)
    i1 = jnp.min(jnp.where(logits == m1, lane, float(LANES)), axis=-1, keepdims=True)
    rest = jnp.where(lane == i1, neg, logits)
    m2 = jnp.max(rest, axis=-1, keepdims=True)
    i2 = jnp.min(jnp.where(rest == m2, lane, float(LANES)), axis=-1, keepdims=True)
    e2 = jnp.exp(m2 - m1)
    w1 = 1.0 / (1.0 + e2)
    w2 = e2 * w1
    o_ref[...] = jnp.where(lane == i1, w1, jnp.where(lane == i2, w2, 0.0))
    sel_ref[...] = jnp.where((lane == i1) | (lane == i2), 1.0, 0.0)


def router(x, g, w_pad, b_pad):
    d = w_pad.shape[0]
    rows, ncol = x.shape[0], x.shape[1] // d
    tm = min(ROW_TILE, rows)
    est = 2 * tm * d * 4 + 2 * d * LANES * 4 + 4 * tm * LANES * 4 + 2 * tm * d * 4
    return pl.pallas_call(
        _router_kernel,
        grid=(rows // tm, ncol),
        in_specs=[
            pl.BlockSpec((tm, d), lambda i, c: (i, c)),
            pl.BlockSpec((1, d), lambda i, c: (0, 0)),
            pl.BlockSpec((d, LANES), lambda i, c: (0, 0)),
            pl.BlockSpec((1, LANES), lambda i, c: (0, 0)),
        ],
        out_specs=(pl.BlockSpec((tm, LANES), lambda i, c: (i, c)), pl.BlockSpec((tm, LANES), lambda i, c: (i, c))),
        out_shape=(jax.ShapeDtypeStruct((rows, ncol * LANES), F32), jax.ShapeDtypeStruct((rows, ncol * LANES), F32)),
        compiler_params=_params(("parallel", "parallel"), est),
        name="router",
    )(x, g, w_pad, b_pad)


def _s5_discretise_kernel(lre_ref, lim_ref, ldt_ref, bre_ref, bim_ref, are_ref, aim_ref, bbre_ref, bbim_ref):
    lam_re = lre_ref[...]
    lam_im = lim_ref[...]
    dt = jnp.exp(ldt_ref[...])
    mag = jnp.exp(lam_re * dt)
    ang = lam_im * dt
    a_re = mag * jnp.cos(ang)
    a_im = mag * jnp.sin(ang)
    den = lam_re * lam_re + lam_im * lam_im
    num_re = a_re - 1.0
    f_re = (num_re * lam_re + a_im * lam_im) / den
    f_im = (a_im * lam_re - num_re * lam_im) / den
    are_ref[...] = a_re
    aim_ref[...] = a_im
    n_slabs, _, width = bre_ref.shape
    for s in range(n_slabs):
        fr = f_re[:, s * width:(s + 1) * width]
        fi = f_im[:, s * width:(s + 1) * width]
        br = bre_ref[s]
        bi = bim_ref[s]
        bbre_ref[s] = (fr * br - fi * bi).astype(BF16)
        bbim_ref[s] = (fr * bi + fi * br).astype(BF16)


def s5_discretise(lam_re_row, lam_im_row, log_dt_row, b_re_blk, b_im_blk):
    n_state = lam_re_row.shape[1]
    shp = b_re_blk.shape
    return pl.pallas_call(
        _s5_discretise_kernel,
        out_shape=(
            jax.ShapeDtypeStruct((1, n_state), F32),
            jax.ShapeDtypeStruct((1, n_state), F32),
            jax.ShapeDtypeStruct(shp, BF16),
            jax.ShapeDtypeStruct(shp, BF16),
        ),
        name="s5_discretise",
    )(lam_re_row, lam_im_row, log_dt_row, b_re_blk, b_im_blk)


def _gelu_tanh(y):
    c = math.sqrt(2.0 / math.pi)
    return 0.5 * y * (1.0 + jnp.tanh(c * (y + 0.044715 * (y * y * y))))


def _s5_kernel(x_ref, g_ref, are_ref, aim_ref, bbre_ref, bbim_ref, cre_ref, cim_ref, d_ref, o_ref,
               sre_ref, sim_ref, stre_ref, stim_ref, h_ref, *, bsz, steps):
    i = pl.program_id(0)

    @pl.when(i == 0)
    def _():
        stre_ref[...] = jnp.zeros_like(stre_ref)
        stim_ref[...] = jnp.zeros_like(stim_ref)

    h_ref[...] = _rms(x_ref[...], g_ref[...])
    n_slabs, slab, width = bbre_ref.shape
    for s in range(n_slabs):
        hs = h_ref[:, s * slab:(s + 1) * slab].astype(BF16)
        sre_ref[:, s * width:(s + 1) * width] = jnp.dot(hs, bbre_ref[s], preferred_element_type=F32)
        sim_ref[:, s * width:(s + 1) * width] = jnp.dot(hs, bbim_ref[s], preferred_element_type=F32)

    n_state = sre_ref.shape[1]
    cw = S5_LANE_CHUNK
    for c in range(n_state // cw):
        lanes = slice(c * cw, (c + 1) * cw)
        ar = jnp.broadcast_to(are_ref[:, lanes], (bsz, cw))
        ai = jnp.broadcast_to(aim_ref[:, lanes], (bsz, cw))

        def step(t, carry):
            xr, xi = carry
            r0 = pl.multiple_of(t * bsz, bsz)
            nr = ar * xr - ai * xi + sre_ref[pl.ds(r0, bsz), lanes]
            ni = ar * xi + ai * xr + sim_ref[pl.ds(r0, bsz), lanes]
            sre_ref[pl.ds(r0, bsz), lanes] = nr
            sim_ref[pl.ds(r0, bsz), lanes] = ni
            return nr, ni

        xr, xi = lax.fori_loop(0, steps, step, (stre_ref[:, lanes], stim_ref[:, lanes]), unroll=8)
        stre_ref[:, lanes] = xr
        stim_ref[:, lanes] = xi

    dsk = d_ref[...]
    for s in range(n_slabs):
        xr = sre_ref[:, s * width:(s + 1) * width].astype(BF16)
        xi = sim_ref[:, s * width:(s + 1) * width].astype(BF16)
        y = (jnp.dot(xr, cre_ref[s], preferred_element_type=F32)
             - jnp.dot(xi, cim_ref[s], preferred_element_type=F32))
        cols = slice(s * slab, (s + 1) * slab)
        y = y + dsk[:, cols] * h_ref[:, cols]
        o_ref[:, cols] = _gelu_tanh(y).astype(o_ref.dtype)


def s5_mixer(x, g, a_re, a_im, bb_re, bb_im, c_re_blk, c_im_blk, d_skip, bsz):
    t, d = x.shape
    seq = t // bsz
    steps = min(S5_STEPS, seq)
    rows = steps * bsz
    n_state = a_re.shape[1]
    wbytes = 2 * (bb_re.size + c_re_blk.size) * 2
    est = 2 * rows * d * 4 + 2 * rows * d * 2 + 2 * wbytes + 2 * rows * n_state * 4 + 4 * rows * 1024 * 4
    const3 = lambda i: (0, 0, 0)
    const2 = lambda i: (0, 0)
    return pl.pallas_call(
        functools.partial(_s5_kernel, bsz=bsz, steps=steps),
        grid=(seq // steps,),
        in_specs=[
            pl.BlockSpec((rows, d), lambda i: (i, 0)),
            pl.BlockSpec((1, d), const2),
            pl.BlockSpec((1, n_state), const2),
            pl.BlockSpec((1, n_state), const2),
            pl.BlockSpec(bb_re.shape, const3),
            pl.BlockSpec(bb_im.shape, const3),
            pl.BlockSpec(c_re_blk.shape, const3),
            pl.BlockSpec(c_im_blk.shape, const3),
            pl.BlockSpec((1, d), const2),
        ],
        out_specs=pl.BlockSpec((rows, d), lambda i: (i, 0)),
        out_shape=jax.ShapeDtypeStruct((t, d), BF16),
        scratch_shapes=[
            pltpu.VMEM((rows, n_state), F32),
            pltpu.VMEM((rows, n_state), F32),
            pltpu.VMEM((bsz, n_state), F32),
            pltpu.VMEM((bsz, n_state), F32),
            pltpu.VMEM((rows, d), F32),
        ],
        compiler_params=_params(("arbitrary",), est),
        name="s5_mixer",
    )(x, g, a_re, a_im, bb_re, bb_im, c_re_blk, c_im_blk, d_skip)


def _mla_proj_kernel(x_ref, g_ref, win_ref, qn_ref, kvn_ref, wa_ref, wb_ref, wk_ref, wv_ref, cos_ref, sin_ref,
                     q_ref, k_ref, v_ref, *, scale):
    h = _rms(x_ref[...], g_ref[...]).astype(BF16)
    c = jnp.dot(h, win_ref[...], preferred_element_type=F32)
    qc = _rms(c[:, :MLA_Q_RANK], qn_ref[...]).astype(BF16)
    kvc = _rms(c[:, MLA_Q_RANK:MLA_Q_RANK + MLA_KV_RANK], kvn_ref[...]).astype(BF16)
    cos = cos_ref[...]
    sin = sin_ref[...]
    base = MLA_Q_RANK + MLA_KV_RANK
    k_rope = c[:, base:base + LANES] * cos + c[:, base + LANES:base + 2 * LANES] * sin
    lane = lax.broadcasted_iota(jnp.int32, (1, LANES), 1)
    q_cos = (cos + jnp.where(lane >= 2 * MLA_ROPE, 1.0, 0.0)) * scale
    q_sin = sin * scale
    k_in = jnp.concatenate([kvc, k_rope.astype(BF16)], axis=-1)
    per = 4
    for cblk in range(MLA_HEADS // per):
        cols = slice(cblk * per * LANES, (cblk + 1) * per * LANES)
        qa = jnp.dot(qc, wa_ref[:, cols], preferred_element_type=F32)
        qb = jnp.dot(qc, wb_ref[:, cols], preferred_element_type=F32)
        kk = jnp.dot(k_in, wk_ref[:, cols], preferred_element_type=F32)
        for jj in range(per):
            sl = slice(jj * LANES, (jj + 1) * LANES)
            q_ref[cblk * per + jj] = (qa[:, sl] * q_cos + qb[:, sl] * q_sin).astype(BF16)
            k_ref[cblk * per + jj] = kk[:, sl].astype(BF16)
    vv = jnp.dot(kvc, wv_ref[...], preferred_element_type=F32)
    for jj in range(MLA_HEADS // 2):
        v_ref[jj] = vv[:, jj * LANES:(jj + 1) * LANES].astype(BF16)


def mla_proj(x, bsz, seq, g, w_in, q_norm, kv_norm, wa, wb, wk, wv, cos_tab, sin_tab):
    d = D_MODEL
    tl = min(ROW_TILE, seq)
    scale = float((MLA_NOPE + MLA_ROPE) ** -0.5) * LOG2_E
    wbytes = (w_in.size + wa.size + wb.size + wk.size + wv.size) * 2
    est = 2 * tl * d * 4 + 2 * wbytes + 2 * 5 * tl * 1024 * 2 + 8 * tl * 1024 * 4
    c2 = lambda i, b: (0, 0)
    nh = MLA_HEADS
    return pl.pallas_call(
        functools.partial(_mla_proj_kernel, scale=scale),
        grid=(seq // tl, bsz),
        in_specs=[
            _x_spec(False, tl, d),
            pl.BlockSpec((1, d), c2),
            pl.BlockSpec(w_in.shape, c2),
            pl.BlockSpec((1, MLA_Q_RANK), c2),
            pl.BlockSpec((1, MLA_KV_RANK), c2),
            pl.BlockSpec(wa.shape, c2),
            pl.BlockSpec(wb.shape, c2),
            pl.BlockSpec(wk.shape, c2),
            pl.BlockSpec(wv.shape, c2),
            pl.BlockSpec((tl, LANES), lambda i, b: (i, 0)),
            pl.BlockSpec((tl, LANES), lambda i, b: (i, 0)),
        ],
        out_specs=(
            pl.BlockSpec((None, nh, tl, LANES), lambda i, b: (b, 0, i, 0)),
            pl.BlockSpec((None, nh, tl, LANES), lambda i, b: (b, 0, i, 0)),
            pl.BlockSpec((None, nh // 2, tl, LANES), lambda i, b: (b, 0, i, 0)),
        ),
        out_shape=(
            jax.ShapeDtypeStruct((bsz, nh, seq, LANES), BF16),
            jax.ShapeDtypeStruct((bsz, nh, seq, LANES), BF16),
            jax.ShapeDtypeStruct((bsz, nh // 2, seq, LANES), BF16),
        ),
        compiler_params=_params(("parallel", "parallel"), est),
        name="mla_proj",
    )(x, g, w_in, q_norm, kv_norm, wa, wb, wk, wv, cos_tab, sin_tab)


def _mla_attn_kernel(q_ref, k_ref, v_ref, o_ref, *, blk, n_blocks):
    lane = lax.broadcasted_iota(jnp.int32, (1, LANES), 1)
    row = lax.broadcasted_iota(jnp.int32, (blk, blk), 0)
    col = lax.broadcasted_iota(jnp.int32, (blk, blk), 1)
    causal = col <= row
    nt = (((1,), (1,)), ((), ()))
    v_pair = v_ref[...]
    v_heads = [jnp.where(lane < MLA_V, v_pair, jnp.zeros_like(v_pair)),
               jnp.where(lane >= MLA_V, v_pair, jnp.zeros_like(v_pair))]
    for qi in range(n_blocks):
        r0 = qi * blk
        probs, vals, inv_l = [], [], []
        for hh in range(2):
            qh = q_ref[hh, r0:r0 + blk, :]
            s_d = lax.dot_general(qh, k_ref[hh, r0:r0 + blk, :], nt, preferred_element_type=F32)
            s_d = jnp.where(causal, s_d, -jnp.inf)
            m = jnp.max(s_d, axis=-1, keepdims=True)
            if qi > 0:
                s_o = lax.dot_general(qh, k_ref[hh, 0:r0, :], nt, preferred_element_type=F32)
                m = jnp.maximum(m, jnp.max(s_o, axis=-1, keepdims=True))
                p_o = jnp.exp2(s_o - m)
                probs.append(p_o.astype(BF16))
            p_d = jnp.exp2(s_d - m)
            probs.append(p_d.astype(BF16))
            l = jnp.sum(p_d, axis=-1, keepdims=True)
            if qi > 0:
                l = l + jnp.sum(p_o, axis=-1, keepdims=True)
            inv_l.append(1.0 / l)
            vals.append(v_heads[hh][0:r0 + blk, :])
        acc = jnp.dot(jnp.concatenate(probs, axis=1), jnp.concatenate(vals, axis=0), preferred_element_type=F32)
        o_ref[r0:r0 + blk, :] = (acc * jnp.where(lane < MLA_V, inv_l[0], inv_l[1])).astype(o_ref.dtype)


def mla_attention(q, k, v, bsz, seq):
    blk = min(ATT_BLOCK, seq)
    n_pairs = MLA_HEADS // 2
    est = 12 * seq * LANES * 2 + 6 * blk * seq * 4
    return pl.pallas_call(
        functools.partial(_mla_attn_kernel, blk=blk, n_blocks=seq // blk),
        grid=(bsz, n_pairs),
        in_specs=[
            pl.BlockSpec((None, 2, seq, LANES), lambda b, h: (b, h, 0, 0)),
            pl.BlockSpec((None, 2, seq, LANES), lambda b, h: (b, h, 0, 0)),
            pl.BlockSpec((None, None, seq, LANES), lambda b, h: (b, h, 0, 0)),
        ],
        out_specs=pl.BlockSpec((seq, LANES), lambda b, h: (0, b * n_pairs + h)),
        out_shape=jax.ShapeDtypeStruct((seq, bsz * D_MODEL), BF16),
        compiler_params=_params(("parallel", "parallel"), est),
        name="mla_attention",
    )(q, k, v)


def _final_norm_kernel(x_ref, g_ref, o_ref):
    o_ref[...] = _rms(x_ref[...], g_ref[...])


def final_norm(x, g, bsz, seq):
    d = D_MODEL
    tl = min(ROW_TILE, seq)
    return pl.pallas_call(
        _final_norm_kernel,
        grid=(seq // tl, bsz),
        in_specs=[_x_spec(False, tl, d), pl.BlockSpec((1, d), lambda i, b: (0, 0))],
        out_specs=pl.BlockSpec((None, tl, d), lambda i, b: (b, i, 0)),
        out_shape=jax.ShapeDtypeStruct((bsz, seq, d), F32),
        compiler_params=_params(("parallel", "parallel"), 8 * tl * d * 4),
        name="final_norm",
    )(x, g)


def _cast_kernel(x_ref, o_ref):
    o_ref[...] = x_ref[...].astype(o_ref.dtype)


def layer_weight_bf16(w_stacked, layer):
    shape = w_stacked.shape[1:]
    n = shape[-1]
    rows = math.prod(shape[:-1])
    tr = 1 << ((CAST_BLOCK_BYTES // (4 * n)).bit_length() - 1)
    while rows % tr:
        tr //= 2
    out = pl.pallas_call(
        _cast_kernel,
        grid=(rows // tr,),
        in_specs=[pl.BlockSpec((None, tr, n), lambda i: (layer, i, 0))],
        out_specs=pl.BlockSpec((tr, n), lambda i: (i, 0)),
        out_shape=jax.ShapeDtypeStruct((rows, n), BF16),
        compiler_params=_params(("parallel",), 2 * tr * n * 6),
        name="weight_bf16",
    )(w_stacked.reshape(w_stacked.shape[0], rows, n))
    return out.reshape(shape)


def _row(v):
    return v.reshape(1, -1).astype(F32)


def _sb_qkv_weight(w_qkv):
    scale = SB_HEAD_DIM ** -0.5 * LOG2_E
    col_scale = jnp.concatenate([jnp.full((D_MODEL,), scale, F32), jnp.ones((2 * D_MODEL,), F32)])
    return (w_qkv * col_scale[None, :]).astype(BF16)


def _s5_block_diag(b_re, b_im, c_re, c_im):
    per = S5_SLAB // S5_GROUP
    n_slabs = S5_GROUPS // per
    eye = jnp.eye(per, dtype=F32)

    def b_blk(b):
        b4 = b.reshape(n_slabs, per, S5_STATE, S5_GROUP)
        return jnp.einsum('sgpc,gh->sgchp', b4, eye).reshape(n_slabs, per * S5_GROUP, per * S5_STATE)

    def c_blk(c):
        c4 = c.reshape(n_slabs, per, S5_GROUP, S5_STATE)
        return jnp.einsum('sgcp,gh->sgphc', c4, eye).reshape(n_slabs, per * S5_STATE, per * S5_GROUP)

    return b_blk(b_re), b_blk(b_im), c_blk(c_re).astype(BF16), c_blk(c_im).astype(BF16)


def _rotate_half_cols(w):
    half = w.shape[-1] // 2
    return jnp.concatenate([-w[..., half:], w[..., :half]], axis=-1)


def _mla_weights(w_in, w_q_b, w_kv_b):
    base = MLA_Q_RANK + MLA_KV_RANK
    w_kr = w_in[:, base:]
    pad = jnp.zeros((D_MODEL, LANES - MLA_ROPE), F32)
    w_in2 = jnp.concatenate([w_in[:, :base], w_kr, pad, _rotate_half_cols(w_kr), pad], axis=-1).astype(BF16)
    wq = w_q_b.reshape(MLA_Q_RANK, MLA_HEADS, MLA_NOPE + MLA_ROPE)
    q_nope, q_rope = wq[..., :MLA_NOPE], wq[..., MLA_NOPE:]
    zq = jnp.zeros((MLA_Q_RANK, MLA_HEADS, MLA_ROPE), F32)
    wa = jnp.concatenate([q_rope, zq, q_nope], axis=-1).reshape(MLA_Q_RANK, MLA_HEADS * LANES).astype(BF16)
    wb = jnp.concatenate([_rotate_half_cols(q_rope), zq, jnp.zeros_like(q_nope)], axis=-1)
    wb = wb.reshape(MLA_Q_RANK, MLA_HEADS * LANES).astype(BF16)
    wkv = w_kv_b.reshape(MLA_KV_RANK, MLA_HEADS, MLA_NOPE + MLA_V)
    k_nope, v_w = wkv[..., :MLA_NOPE], wkv[..., MLA_NOPE:]
    zk = jnp.zeros((MLA_KV_RANK, MLA_HEADS, 2 * MLA_ROPE), F32)
    wk_top = jnp.concatenate([zk, k_nope], axis=-1).reshape(MLA_KV_RANK, MLA_HEADS * LANES)
    route = jnp.concatenate([jnp.eye(LANES, MLA_ROPE, dtype=F32), jnp.zeros((LANES, LANES - MLA_ROPE), F32)], axis=-1)
    wk_bot = jnp.tile(route, (1, MLA_HEADS))
    wk = jnp.concatenate([wk_top, wk_bot], axis=0).astype(BF16)
    wv = v_w.reshape(MLA_KV_RANK, MLA_HEADS * MLA_V).astype(BF16)
    return w_in2, wa, wb, wk, wv


def _rope_tables(seq, bsz):
    pos = jnp.arange(seq, dtype=F32)
    inv_freq = ROPE_THETA ** (-jnp.arange(0, MLA_ROPE, 2, dtype=F32) / MLA_ROPE)
    ang = pos[:, None] * inv_freq[None, :]
    zeros = jnp.zeros((seq, LANES - MLA_ROPE), F32)
    cos = jnp.concatenate([jnp.cos(ang), jnp.cos(ang), zeros], axis=-1)
    sin = jnp.concatenate([jnp.sin(ang), jnp.sin(ang), zeros], axis=-1)
    return cos, sin


def moe_layer(flat, g_ffn, w_router, b_router, w_gate_up, w_down):
    w_r = jnp.pad(w_router, ((0, 0), (0, LANES - N_EXPERTS)))
    b_r = jnp.pad(b_router, (0, LANES - N_EXPERTS)).reshape(1, LANES)
    gates, sel = router(flat, g_ffn, w_r, b_r)
    return moe_ffn(flat, g_ffn, w_gate_up, w_down, gates, sel)


def kernel(x, norm_mix, norm_ffn, final_norm_g, sb_w_qkv, sb_w_o, s5_lambda_re, s5_lambda_im, s5_log_dt, s5_b_re,
           s5_b_im, s5_c_re, s5_c_im, s5_d, s5_w_glu, mla_w_in, mla_q_norm, mla_w_q_b, mla_kv_norm, mla_w_kv_b,
           mla_w_o, ffn_w_gate_up, ffn_w_down, moe_w_router, moe_b_router, moe_w_gate_up, moe_w_down):
    bsz, seq, d = x.shape
    t = bsz * seq
    blk = min(ATT_BLOCK, seq)
    tri = (jnp.arange(blk)[:, None] > jnp.arange(blk)[None, :]).astype(BF16)
    cos_tab, sin_tab = _rope_tables(seq, bsz)

    cur = x
    is_bld = True

    def wide(a):
        return a if a.shape[0] == seq else a.reshape(seq, bsz * d)

    for i in range(DEPTH):
        kind, slot = i % N_MIXERS, i // N_MIXERS
        g_mix = _row(norm_mix[i])
        if kind == 0:
            cur = cur if is_bld else wide(cur)
            qkv = norm_proj_blocks(cur, is_bld, bsz, seq, g_mix, _sb_qkv_weight(sb_w_qkv[slot]))
            o = sb_attention(qkv, bsz, seq, tri)
            cur = proj_residual(o, cur, is_bld, seq, bsz, sb_w_o[slot].astype(BF16))
        elif kind == 1:
            b_re_blk, b_im_blk, c_re_blk, c_im_blk = _s5_block_diag(
                s5_b_re[slot], s5_b_im[slot], s5_c_re[slot], s5_c_im[slot])
            log_dt_row = jnp.repeat(s5_log_dt[slot], S5_STATE).reshape(1, -1)
            a_re, a_im, bb_re, bb_im = s5_discretise(
                _row(s5_lambda_re[slot]), _row(s5_lambda_im[slot]), log_dt_row, b_re_blk, b_im_blk)
            if is_bld:
                cur = jnp.transpose(cur, (1, 0, 2))
            flat = cur.reshape(t, d)
            y = s5_mixer(flat, g_mix, a_re, a_im, bb_re, bb_im, c_re_blk, c_im_blk, _row(s5_d[slot]), bsz)
            cur = proj_residual(y, flat, False, t, 1, s5_w_glu[slot].astype(BF16), glu=True)
        else:
            if is_bld:
                cur = jnp.transpose(cur, (1, 0, 2))
            cur = wide(cur)
            w_in2, wa, wb, wk, wv = _mla_weights(mla_w_in[slot], mla_w_q_b[slot], mla_w_kv_b[slot])
            q, k, v = mla_proj(cur, bsz, seq, g_mix, w_in2, _row(mla_q_norm[slot]), _row(mla_kv_norm[slot]),
                               wa, wb, wk, wv, cos_tab, sin_tab)
            o = mla_attention(q, k, v, bsz, seq)
            cur = proj_residual(o, cur, False, seq, bsz, mla_w_o[slot].astype(BF16))
        is_bld = False
        g_ffn = _row(norm_ffn[i])
        if i % 2 == 0:
            cur = ffn(cur, g_ffn, layer_weight_bf16(ffn_w_gate_up, i // 2), layer_weight_bf16(ffn_w_down, i // 2))
        else:
            cur = moe_layer(cur, g_ffn, moe_w_router[i // 2], moe_b_router[i // 2],
                            layer_weight_bf16(moe_w_gate_up, i // 2), layer_weight_bf16(moe_w_down, i // 2))
    return final_norm(wide(cur), _row(final_norm_g), bsz, seq)
```

```python
import functools
import math

import jax
import jax.numpy as jnp
from jax import lax
from jax.experimental import pallas as pl
from jax.experimental.pallas import tpu as pltpu

F32 = jnp.float32
BF16 = jnp.bfloat16

D_MODEL = 1024
DEPTH = 4
N_MIXERS = 3
RMS_EPS = 1e-6
LOG2_E = 1.0 / math.log(2.0)
SB_HEADS = 16
SB_HEAD_DIM = D_MODEL // SB_HEADS
S5_GROUP = 16
S5_GROUPS = D_MODEL // S5_GROUP
S5_STATE = 64
MLA_HEADS = 16
MLA_NOPE = 64
MLA_ROPE = 32
MLA_V = 64
MLA_Q_RANK = 384
MLA_KV_RANK = 256
ROPE_THETA = 10000.0
D_FF = 2816
N_EXPERTS = 8
TOP_K = 2
D_FF_EXPERT = 1408

LANES = 128
SUBLANES = 8
MXU_DIM = 256
VMEM_BYTES_V7X = 64 * 1024 * 1024
VMEM_CAP = VMEM_BYTES_V7X - 8 * 1024 * 1024

ROW_TILE = 512
ATT_BLOCK = MXU_DIM
S5_STEPS = 64
S5_SLAB = MXU_DIM
S5_LANE_CHUNK = 512
FF_CHUNK = 1408
MOE_ROW_TILE = 1024
MOE_CAPACITY = 256
MOE_EXTRA = 128
CAST_BLOCK_BYTES = 4 * 1024 * 1024


def _params(semantics, est_bytes):
    limit = int(min(VMEM_CAP, max(32 * 1024 * 1024, est_bytes * 5 // 4)))
    return pltpu.CompilerParams(dimension_semantics=semantics, vmem_limit_bytes=limit)


def _rms(xf, g):
    ms = jnp.mean(xf * xf, axis=-1, keepdims=True)
    return xf * lax.rsqrt(ms + RMS_EPS) * g


def _sigmoid(v):
    return 1.0 / (1.0 + jnp.exp(-v))


def _x_spec(x_is_bld, tl, d):
    if x_is_bld:
        return pl.BlockSpec((None, tl, d), lambda i, b: (b, i, 0))
    return pl.BlockSpec((tl, d), lambda i, b: (i, b))


def _norm_proj_blocks_kernel(x_ref, g_ref, w_ref, o_ref):
    h = _rms(x_ref[...], g_ref[...]).astype(BF16)
    nb = o_ref.shape[0]
    per = 4
    for c in range(nb // per):
        r = jnp.dot(h, w_ref[:, c * per * LANES:(c + 1) * per * LANES], preferred_element_type=F32)
        for jj in range(per):
            o_ref[c * per + jj] = r[:, jj * LANES:(jj + 1) * LANES].astype(BF16)


def norm_proj_blocks(x, x_is_bld, bsz, seq, g, w):
    d, n = w.shape
    nb = n // LANES
    tl = min(ROW_TILE, seq)
    est = 2 * tl * d * 4 + 2 * d * n * 2 + 2 * n * tl * 2 + 4 * tl * 512 * 4
    return pl.pallas_call(
        _norm_proj_blocks_kernel,
        grid=(seq // tl, bsz),
        in_specs=[
            _x_spec(x_is_bld, tl, d),
            pl.BlockSpec((1, d), lambda i, b: (0, 0)),
            pl.BlockSpec((d, n), lambda i, b: (0, 0)),
        ],
        out_specs=pl.BlockSpec((nb, None, tl, LANES), lambda i, b: (0, b, i, 0)),
        out_shape=jax.ShapeDtypeStruct((nb, bsz, seq, LANES), BF16),
        compiler_params=_params(("parallel", "parallel"), est),
        name="norm_proj_blocks",
    )(x, g, w)


def _sb_attn_kernel(q_ref, k_ref, v_ref, u_ref, o_ref, *, blk, n_blocks):
    lane = lax.broadcasted_iota(jnp.int32, (1, LANES), 1)
    row = lax.broadcasted_iota(jnp.int32, (blk, blk), 0)
    col = lax.broadcasted_iota(jnp.int32, (blk, blk), 1)
    strict = col < row
    u = u_ref[...]
    nt = (((1,), (1,)), ((), ()))

    def neg_log2_one_minus_beta(z):
        return jnp.maximum(z, 0.0) + jnp.log2(1.0 + jnp.exp2(-jnp.abs(z)))

    v_pair = v_ref[...]
    v_heads = [jnp.where(lane < SB_HEAD_DIM, v_pair, jnp.zeros_like(v_pair)),
               jnp.where(lane >= SB_HEAD_DIM, v_pair, jnp.zeros_like(v_pair))]
    for qi in range(n_blocks):
        r0 = qi * blk
        q = q_ref[r0:r0 + blk, :]
        weights, vals = [], []
        for hh in range(2):
            head_lanes = (lane >= SB_HEAD_DIM) if hh else (lane < SB_HEAD_DIM)
            qh = jnp.where(head_lanes, q, jnp.zeros_like(q))
            z = lax.dot_general(qh, k_ref[r0:r0 + blk, :], nt, preferred_element_type=F32)
            nl = jnp.where(strict, neg_log2_one_minus_beta(z), 0.0)
            tail = jnp.dot(nl.astype(BF16), u, preferred_element_type=F32)
            w = jnp.where(strict, jnp.exp2((z - nl) - tail), 0.0)
            weights.append(w.astype(BF16))
            vals.append(v_heads[hh][r0:r0 + blk, :])
            run = jnp.sum(nl, axis=-1, keepdims=True)
            if qi > 0:
                for j in range(qi - 1, -1, -1):
                    z = lax.dot_general(qh, k_ref[j * blk:(j + 1) * blk, :], nt, preferred_element_type=F32)
                    nl = neg_log2_one_minus_beta(z)
                    tail = jnp.dot(nl.astype(BF16), u, preferred_element_type=F32) + run
                    weights.append(jnp.exp2((z - nl) - tail).astype(BF16))
                    vals.append(v_heads[hh][j * blk:(j + 1) * blk, :])
                    if j > 0:
                        run = run + jnp.sum(nl, axis=-1, keepdims=True)
        acc = jnp.dot(jnp.concatenate(weights, axis=1), jnp.concatenate(vals, axis=0), preferred_element_type=F32)
        o_ref[r0:r0 + blk, :] = acc.astype(o_ref.dtype)


def sb_attention(qkv, bsz, seq, tri):
    blk = min(ATT_BLOCK, seq)
    n_pairs = SB_HEADS // 2
    est = 8 * seq * LANES * 2 + 6 * blk * seq * 4
    return pl.pallas_call(
        functools.partial(_sb_attn_kernel, blk=blk, n_blocks=seq // blk),
        grid=(bsz, n_pairs),
        in_specs=[
            pl.BlockSpec((None, None, seq, LANES), lambda b, h: (h, b, 0, 0)),
            pl.BlockSpec((None, None, seq, LANES), lambda b, h: (n_pairs + h, b, 0, 0)),
            pl.BlockSpec((None, None, seq, LANES), lambda b, h: (2 * n_pairs + h, b, 0, 0)),
            pl.BlockSpec((blk, blk), lambda b, h: (0, 0)),
        ],
        out_specs=pl.BlockSpec((seq, LANES), lambda b, h: (0, b * n_pairs + h)),
        out_shape=jax.ShapeDtypeStruct((seq, bsz * D_MODEL), BF16),
        compiler_params=_params(("parallel", "parallel"), est),
        name="sb_attention",
    )(qkv, qkv, qkv, tri)


def _proj_residual_kernel(a_ref, x_ref, w_ref, o_ref, *, glu):
    r = jnp.dot(a_ref[...], w_ref[...], preferred_element_type=F32)
    if glu:
        n = o_ref.shape[-1]
        r = r[:, :n] * _sigmoid(r[:, n:])
    o_ref[...] = x_ref[...] + r


def proj_residual(a, x, x_is_bld, rows, ncol, w, glu=False):
    k, n_w = w.shape
    n = n_w // 2 if glu else n_w
    tl = min(ROW_TILE, rows)
    est = 2 * tl * k * 2 + 4 * tl * n * 4 + 2 * k * n_w * 2 + 2 * tl * n_w * 4
    return pl.pallas_call(
        functools.partial(_proj_residual_kernel, glu=glu),
        grid=(rows // tl, ncol),
        in_specs=[
            pl.BlockSpec((tl, k), lambda i, b: (i, b)),
            _x_spec(x_is_bld, tl, n),
            pl.BlockSpec((k, n_w), lambda i, b: (0, 0)),
        ],
        out_specs=pl.BlockSpec((tl, n), lambda i, b: (i, b)),
        out_shape=jax.ShapeDtypeStruct((rows, ncol * n), F32),
        compiler_params=_params(("parallel", "parallel"), est),
        name="proj_residual_glu" if glu else "proj_residual",
    )(a, x, w)


def _ffn_kernel(x_ref, g_ref, wg_ref, wu_ref, wd_ref, o_ref, h_ref):
    j = pl.program_id(2)

    @pl.when(j == 0)
    def _():
        xf = x_ref[...]
        h_ref[...] = _rms(xf, g_ref[...]).astype(BF16)
        o_ref[...] = xf

    h = h_ref[...]
    gv = jnp.dot(h, wg_ref[...], preferred_element_type=F32)
    uv = jnp.dot(h, wu_ref[...], preferred_element_type=F32)
    act = gv * _sigmoid(gv) * uv
    o_ref[...] += jnp.dot(act.astype(BF16), wd_ref[...], preferred_element_type=F32)


def ffn(x, g, w_gate_up, w_down):
    f, d = w_down.shape
    rows, ncol = x.shape[0], x.shape[1] // d
    tf = FF_CHUNK
    n_chunks = f // tf
    tm = min(ROW_TILE, rows)
    est = 4 * tm * d * 4 + tm * d * 2 + 6 * d * tf * 2 + 3 * tm * tf * 4
    return pl.pallas_call(
        _ffn_kernel,
        grid=(rows // tm, ncol, n_chunks),
        in_specs=[
            pl.BlockSpec((tm, d), lambda i, c, j: (i, c)),
            pl.BlockSpec((1, d), lambda i, c, j: (0, 0)),
            pl.BlockSpec((d, tf), lambda i, c, j: (0, j)),
            pl.BlockSpec((d, tf), lambda i, c, j: (0, n_chunks + j)),
            pl.BlockSpec((tf, d), lambda i, c, j: (j, 0)),
        ],
        out_specs=pl.BlockSpec((tm, d), lambda i, c, j: (i, c)),
        out_shape=jax.ShapeDtypeStruct(x.shape, F32),
        scratch_shapes=[pltpu.VMEM((tm, d), BF16)],
        compiler_params=_params(("parallel", "parallel", "arbitrary"), est),
        name="ffn",
    )(x, g, w_gate_up, w_gate_up, w_down)


def _moe_ffn_kernel(x_ref, g_ref, gates_ref, sel_ref, wg_ref, wu_ref, wd_ref, o_ref, h_ref, rank_ref, *, cap, extra):
    e = pl.program_id(2)
    tm, d = x_ref.shape
    lane = lax.broadcasted_iota(jnp.int32, (1, LANES), 1)

    @pl.when(e == 0)
    def _():
        xf = x_ref[...]
        h_ref[...] = _rms(xf, g_ref[...]).astype(BF16)
        o_ref[...] = xf
        r_i = lax.broadcasted_iota(jnp.int32, (tm, tm), 0)
        c_i = lax.broadcasted_iota(jnp.int32, (tm, tm), 1)
        earlier = jnp.where(c_i < r_i, 1.0, 0.0).astype(BF16)
        rank_ref[...] = jnp.dot(earlier, sel_ref[...].astype(BF16), preferred_element_type=F32)

    mine = lane == e
    sel_e = jnp.sum(jnp.where(mine, sel_ref[...], 0.0), axis=-1, keepdims=True)
    rank_e = jnp.sum(jnp.where(mine, rank_ref[...], 0.0), axis=-1, keepdims=True)
    gate_e = jnp.sum(jnp.where(mine, gates_ref[...], 0.0), axis=-1, keepdims=True)
    count = jnp.sum(sel_e).astype(jnp.int32)
    half = d // 2

    def run_slots(first, width):
        slot = lax.broadcasted_iota(jnp.int32, (1, width), 1).astype(F32)
        pick = jnp.where((rank_e - first == slot) & (sel_e > 0.5), 1.0, 0.0).astype(BF16)
        packed = lax.dot_general(pick, h_ref[...], (((0,), (0,)), ((), ())), preferred_element_type=F32)
        packed = packed.astype(BF16)
        gv = jnp.dot(packed, wg_ref[...], preferred_element_type=F32)
        uv = jnp.dot(packed, wu_ref[...], preferred_element_type=F32)
        act = (gv * _sigmoid(gv) * uv).astype(BF16)
        y = jnp.dot(act, wd_ref[...], preferred_element_type=F32).astype(BF16)
        for c0 in (0, half):
            spread = jnp.dot(pick, y[:, c0:c0 + half], preferred_element_type=F32)
            o_ref[:, c0:c0 + half] += gate_e * spread

    @pl.when(count > 0)
    def _():
        run_slots(jnp.float32(0.0), cap)

    def extra_pass(p, carry):
        run_slots((cap + p * extra).astype(F32), extra)
        return carry

    n_extra = (jnp.maximum(count - cap, 0) + (extra - 1)) // extra
    lax.fori_loop(0, n_extra, extra_pass, 0)


def moe_ffn(x, g, w_gate_up, w_down, gates, sel):
    n_e, f, d = w_down.shape
    rows, ncol = x.shape[0], x.shape[1] // d
    tm = min(MOE_ROW_TILE, rows)
    cap = min(MOE_CAPACITY, tm)
    extra = min(MOE_EXTRA, tm)
    est = (4 * tm * d * 4 + tm * d * 2 + 6 * d * f * 2 + 5 * tm * LANES * 4
           + 2 * tm * tm * 2 + 3 * cap * f * 4 + 2 * cap * d * 4 + 2 * tm * d * 4)
    return pl.pallas_call(
        functools.partial(_moe_ffn_kernel, cap=cap, extra=extra),
        grid=(rows // tm, ncol, n_e),
        in_specs=[
            pl.BlockSpec((tm, d), lambda i, c, e: (i, c)),
            pl.BlockSpec((1, d), lambda i, c, e: (0, 0)),
            pl.BlockSpec((tm, LANES), lambda i, c, e: (i, c)),
            pl.BlockSpec((tm, LANES), lambda i, c, e: (i, c)),
            pl.BlockSpec((None, d, f), lambda i, c, e: (e, 0, 0)),
            pl.BlockSpec((None, d, f), lambda i, c, e: (e, 0, 1)),
            pl.BlockSpec((None, f, d), lambda i, c, e: (e, 0, 0)),
        ],
        out_specs=pl.BlockSpec((tm, d), lambda i, c, e: (i, c)),
        out_shape=jax.ShapeDtypeStruct(x.shape, F32),
        scratch_shapes=[pltpu.VMEM((tm, d), BF16), pltpu.VMEM((tm, LANES), F32)],
        compiler_params=_params(("parallel", "parallel", "arbitrary"), est),
        name="moe_ffn",
    )(x, g, gates, sel, w_gate_up, w_gate_up, w_down)


def _router_kernel(x_ref, g_ref, w_ref, b_ref, o_ref, sel_ref):
    h = _rms(x_ref[...], g_ref[...])
    logits = jnp.dot(h, w_ref[...], preferred_element_type=F32, precision=lax.Precision.HIGHEST) + b_ref[...]
    lane = lax.broadcasted_iota(jnp.int32, logits.shape, 1).astype(F32)
    neg = jnp.float32(-jnp.inf)
    logits = jnp.where(lane < N_EXPERTS, logits, neg)
    m1 = jnp.max(logits, axis=-1, keepdims=True)
    i1 = jnp.min(jnp.where(logits == m1, lane, float(LANES)), axis=-1, keepdims=True)
    rest = jnp.where(lane == i1, neg, logits)
    m2 = jnp.max(rest, axis=-1, keepdims=True)
    i2 = jnp.min(jnp.where(rest == m2, lane, float(LANES)), axis=-1, keepdims=True)
    e2 = jnp.exp(m2 - m1)
    w1 = 1.0 / (1.0 + e2)
    w2 = e2 * w1
    o_ref[...] = jnp.where(lane == i1, w1, jnp.where(lane == i2, w2, 0.0))
    sel_ref[...] = jnp.where((lane == i1) | (lane == i2), 1.0, 0.0)


def router(x, g, w_pad, b_pad):
    d = w_pad.shape[0]
    rows, ncol = x.shape[0], x.shape[1] // d
    tm = min(ROW_TILE, rows)
    est = 2 * tm * d * 4 + 2 * d * LANES * 4 + 4 * tm * LANES * 4 + 2 * tm * d * 4
    return pl.pallas_call(
        _router_kernel,
        grid=(rows // tm, ncol),
        in_specs=[
            pl.BlockSpec((tm, d), lambda i, c: (i, c)),
            pl.BlockSpec((1, d), lambda i, c: (0, 0)),
            pl.BlockSpec((d, LANES), lambda i, c: (0, 0)),
            pl.BlockSpec((1, LANES), lambda i, c: (0, 0)),
        ],
        out_specs=(pl.BlockSpec((tm, LANES), lambda i, c: (i, c)), pl.BlockSpec((tm, LANES), lambda i, c: (i, c))),
        out_shape=(jax.ShapeDtypeStruct((rows, ncol * LANES), F32), jax.ShapeDtypeStruct((rows, ncol * LANES), F32)),
        compiler_params=_params(("parallel", "parallel"), est),
        name="router",
    )(x, g, w_pad, b_pad)


def _s5_discretise_kernel(lre_ref, lim_ref, ldt_ref, bre_ref, bim_ref, are_ref, aim_ref, bbre_ref, bbim_ref):
    lam_re = lre_ref[...]
    lam_im = lim_ref[...]
    dt = jnp.exp(ldt_ref[...])
    mag = jnp.exp(lam_re * dt)
    ang = lam_im * dt
    a_re = mag * jnp.cos(ang)
    a_im = mag * jnp.sin(ang)
    den = lam_re * lam_re + lam_im * lam_im
    num_re = a_re - 1.0
    f_re = (num_re * lam_re + a_im * lam_im) / den
    f_im = (a_im * lam_re - num_re * lam_im) / den
    are_ref[...] = a_re
    aim_ref[...] = a_im
    n_slabs, _, width = bre_ref.shape
    for s in range(n_slabs):
        fr = f_re[:, s * width:(s + 1) * width]
        fi = f_im[:, s * width:(s + 1) * width]
        br = bre_ref[s]
        bi = bim_ref[s]
        bbre_ref[s] = (fr * br - fi * bi).astype(BF16)
        bbim_ref[s] = (fr * bi + fi * br).astype(BF16)


def s5_discretise(lam_re_row, lam_im_row, log_dt_row, b_re_blk, b_im_blk):
    n_state = lam_re_row.shape[1]
    shp = b_re_blk.shape
    return pl.pallas_call(
        _s5_discretise_kernel,
        out_shape=(
            jax.ShapeDtypeStruct((1, n_state), F32),
            jax.ShapeDtypeStruct((1, n_state), F32),
            jax.ShapeDtypeStruct(shp, BF16),
            jax.ShapeDtypeStruct(shp, BF16),
        ),
        name="s5_discretise",
    )(lam_re_row, lam_im_row, log_dt_row, b_re_blk, b_im_blk)


def _gelu_tanh(y):
    c = math.sqrt(2.0 / math.pi)
    return 0.5 * y * (1.0 + jnp.tanh(c * (y + 0.044715 * (y * y * y))))


def _s5_kernel(x_ref, g_ref, are_ref, aim_ref, bbre_ref, bbim_ref, cre_ref, cim_ref, d_ref, o_ref,
               sre_ref, sim_ref, stre_ref, stim_ref, h_ref, *, bsz, steps):
    i = pl.program_id(0)

    @pl.when(i == 0)
    def _():
        stre_ref[...] = jnp.zeros_like(stre_ref)
        stim_ref[...] = jnp.zeros_like(stim_ref)

    h_ref[...] = _rms(x_ref[...], g_ref[...])
    n_slabs, slab, width = bbre_ref.shape
    for s in range(n_slabs):
        hs = h_ref[:, s * slab:(s + 1) * slab].astype(BF16)
        sre_ref[:, s * width:(s + 1) * width] = jnp.dot(hs, bbre_ref[s], preferred_element_type=F32)
        sim_ref[:, s * width:(s + 1) * width] = jnp.dot(hs, bbim_ref[s], preferred_element_type=F32)

    n_state = sre_ref.shape[1]
    cw = S5_LANE_CHUNK
    for c in range(n_state // cw):
        lanes = slice(c * cw, (c + 1) * cw)
        ar = jnp.broadcast_to(are_ref[:, lanes], (bsz, cw))
        ai = jnp.broadcast_to(aim_ref[:, lanes], (bsz, cw))

        def step(t, carry):
            xr, xi = carry
            r0 = pl.multiple_of(t * bsz, bsz)
            nr = ar * xr - ai * xi + sre_ref[pl.ds(r0, bsz), lanes]
            ni = ar * xi + ai * xr + sim_ref[pl.ds(r0, bsz), lanes]
            sre_ref[pl.ds(r0, bsz), lanes] = nr
            sim_ref[pl.ds(r0, bsz), lanes] = ni
            return nr, ni

        xr, xi = lax.fori_loop(0, steps, step, (stre_ref[:, lanes], stim_ref[:, lanes]), unroll=8)
        stre_ref[:, lanes] = xr
        stim_ref[:, lanes] = xi

    dsk = d_ref[...]
    for s in range(n_slabs):
        xr = sre_ref[:, s * width:(s + 1) * width].astype(BF16)
        xi = sim_ref[:, s * width:(s + 1) * width].astype(BF16)
        y = (jnp.dot(xr, cre_ref[s], preferred_element_type=F32)
             - jnp.dot(xi, cim_ref[s], preferred_element_type=F32))
        cols = slice(s * slab, (s + 1) * slab)
        y = y + dsk[:, cols] * h_ref[:, cols]
        o_ref[:, cols] = _gelu_tanh(y).astype(o_ref.dtype)


def s5_mixer(x, g, a_re, a_im, bb_re, bb_im, c_re_blk, c_im_blk, d_skip, bsz):
    t, d = x.shape
    seq = t // bsz
    steps = min(S5_STEPS, seq)
    rows = steps * bsz
    n_state = a_re.shape[1]
    wbytes = 2 * (bb_re.size + c_re_blk.size) * 2
    est = 2 * rows * d * 4 + 2 * rows * d * 2 + 2 * wbytes + 2 * rows * n_state * 4 + 4 * rows * 1024 * 4
    const3 = lambda i: (0, 0, 0)
    const2 = lambda i: (0, 0)
    return pl.pallas_call(
        functools.partial(_s5_kernel, bsz=bsz, steps=steps),
        grid=(seq // steps,),
        in_specs=[
            pl.BlockSpec((rows, d), lambda i: (i, 0)),
            pl.BlockSpec((1, d), const2),
            pl.BlockSpec((1, n_state), const2),
            pl.BlockSpec((1, n_state), const2),
            pl.BlockSpec(bb_re.shape, const3),
            pl.BlockSpec(bb_im.shape, const3),
            pl.BlockSpec(c_re_blk.shape, const3),
            pl.BlockSpec(c_im_blk.shape, const3),
            pl.BlockSpec((1, d), const2),
        ],
        out_specs=pl.BlockSpec((rows, d), lambda i: (i, 0)),
        out_shape=jax.ShapeDtypeStruct((t, d), BF16),
        scratch_shapes=[
            pltpu.VMEM((rows, n_state), F32),
            pltpu.VMEM((rows, n_state), F32),
            pltpu.VMEM((bsz, n_state), F32),
            pltpu.VMEM((bsz, n_state), F32),
            pltpu.VMEM((rows, d), F32),
        ],
        compiler_params=_params(("arbitrary",), est),
        name="s5_mixer",
    )(x, g, a_re, a_im, bb_re, bb_im, c_re_blk, c_im_blk, d_skip)


def _mla_proj_kernel(x_ref, g_ref, win_ref, qn_ref, kvn_ref, wa_ref, wb_ref, wk_ref, wv_ref, cos_ref, sin_ref,
                     q_ref, k_ref, v_ref, *, scale):
    h = _rms(x_ref[...], g_ref[...]).astype(BF16)
    c = jnp.dot(h, win_ref[...], preferred_element_type=F32)
    qc = _rms(c[:, :MLA_Q_RANK], qn_ref[...]).astype(BF16)
    kvc = _rms(c[:, MLA_Q_RANK:MLA_Q_RANK + MLA_KV_RANK], kvn_ref[...]).astype(BF16)
    cos = cos_ref[...]
    sin = sin_ref[...]
    base = MLA_Q_RANK + MLA_KV_RANK
    k_rope = c[:, base:base + LANES] * cos + c[:, base + LANES:base + 2 * LANES] * sin
    lane = lax.broadcasted_iota(jnp.int32, (1, LANES), 1)
    q_cos = (cos + jnp.where(lane >= 2 * MLA_ROPE, 1.0, 0.0)) * scale
    q_sin = sin * scale
    k_in = jnp.concatenate([kvc, k_rope.astype(BF16)], axis=-1)
    per = 4
    for cblk in range(MLA_HEADS // per):
        cols = slice(cblk * per * LANES, (cblk + 1) * per * LANES)
        qa = jnp.dot(qc, wa_ref[:, cols], preferred_element_type=F32)
        qb = jnp.dot(qc, wb_ref[:, cols], preferred_element_type=F32)
        kk = jnp.dot(k_in, wk_ref[:, cols], preferred_element_type=F32)
        for jj in range(per):
            sl = slice(jj * LANES, (jj + 1) * LANES)
            q_ref[cblk * per + jj] = (qa[:, sl] * q_cos + qb[:, sl] * q_sin).astype(BF16)
            k_ref[cblk * per + jj] = kk[:, sl].astype(BF16)
    vv = jnp.dot(kvc, wv_ref[...], preferred_element_type=F32)
    for jj in range(MLA_HEADS // 2):
        v_ref[jj] = vv[:, jj * LANES:(jj + 1) * LANES].astype(BF16)


def mla_proj(x, bsz, seq, g, w_in, q_norm, kv_norm, wa, wb, wk, wv, cos_tab, sin_tab):
    d = D_MODEL
    tl = min(ROW_TILE, seq)
    scale = float((MLA_NOPE + MLA_ROPE) ** -0.5) * LOG2_E
    wbytes = (w_in.size + wa.size + wb.size + wk.size + wv.size) * 2
    est = 2 * tl * d * 4 + 2 * wbytes + 2 * 5 * tl * 1024 * 2 + 8 * tl * 1024 * 4
    c2 = lambda i, b: (0, 0)
    nh = MLA_HEADS
    return pl.pallas_call(
        functools.partial(_mla_proj_kernel, scale=scale),
        grid=(seq // tl, bsz),
        in_specs=[
            _x_spec(False, tl, d),
            pl.BlockSpec((1, d), c2),
            pl.BlockSpec(w_in.shape, c2),
            pl.BlockSpec((1, MLA_Q_RANK), c2),
            pl.BlockSpec((1, MLA_KV_RANK), c2),
            pl.BlockSpec(wa.shape, c2),
            pl.BlockSpec(wb.shape, c2),
            pl.BlockSpec(wk.shape, c2),
            pl.BlockSpec(wv.shape, c2),
            pl.BlockSpec((tl, LANES), lambda i, b: (i, 0)),
            pl.BlockSpec((tl, LANES), lambda i, b: (i, 0)),
        ],
        out_specs=(
            pl.BlockSpec((None, nh, tl, LANES), lambda i, b: (b, 0, i, 0)),
            pl.BlockSpec((None, nh, tl, LANES), lambda i, b: (b, 0, i, 0)),
            pl.BlockSpec((None, nh // 2, tl, LANES), lambda i, b: (b, 0, i, 0)),
        ),
        out_shape=(
            jax.ShapeDtypeStruct((bsz, nh, seq, LANES), BF16),
            jax.ShapeDtypeStruct((bsz, nh, seq, LANES), BF16),
            jax.ShapeDtypeStruct((bsz, nh // 2, seq, LANES), BF16),
        ),
        compiler_params=_params(("parallel", "parallel"), est),
        name="mla_proj",
    )(x, g, w_in, q_norm, kv_norm, wa, wb, wk, wv, cos_tab, sin_tab)


def _mla_attn_kernel(q_ref, k_ref, v_ref, o_ref, *, blk, n_blocks):
    lane = lax.broadcasted_iota(jnp.int32, (1, LANES), 1)
    row = lax.broadcasted_iota(jnp.int32, (blk, blk), 0)
    col = lax.broadcasted_iota(jnp.int32, (blk, blk), 1)
    causal = col <= row
    nt = (((1,), (1,)), ((), ()))
    v_pair = v_ref[...]
    v_heads = [jnp.where(lane < MLA_V, v_pair, jnp.zeros_like(v_pair)),
               jnp.where(lane >= MLA_V, v_pair, jnp.zeros_like(v_pair))]
    for qi in range(n_blocks):
        r0 = qi * blk
        probs, vals, inv_l = [], [], []
        for hh in range(2):
            qh = q_ref[hh, r0:r0 + blk, :]
            s_d = lax.dot_general(qh, k_ref[hh, r0:r0 + blk, :], nt, preferred_element_type=F32)
            s_d = jnp.where(causal, s_d, -jnp.inf)
            m = jnp.max(s_d, axis=-1, keepdims=True)
            if qi > 0:
                s_o = lax.dot_general(qh, k_ref[hh, 0:r0, :], nt, preferred_element_type=F32)
                m = jnp.maximum(m, jnp.max(s_o, axis=-1, keepdims=True))
                p_o = jnp.exp2(s_o - m)
                probs.append(p_o.astype(BF16))
            p_d = jnp.exp2(s_d - m)
            probs.append(p_d.astype(BF16))
            l = jnp.sum(p_d, axis=-1, keepdims=True)
            if qi > 0:
                l = l + jnp.sum(p_o, axis=-1, keepdims=True)
            inv_l.append(1.0 / l)
            vals.append(v_heads[hh][0:r0 + blk, :])
        acc = jnp.dot(jnp.concatenate(probs, axis=1), jnp.concatenate(vals, axis=0), preferred_element_type=F32)
        o_ref[r0:r0 + blk, :] = (acc * jnp.where(lane < MLA_V, inv_l[0], inv_l[1])).astype(o_ref.dtype)


def mla_attention(q, k, v, bsz, seq):
    blk = min(ATT_BLOCK, seq)
    n_pairs = MLA_HEADS // 2
    est = 12 * seq * LANES * 2 + 6 * blk * seq * 4
    return pl.pallas_call(
        functools.partial(_mla_attn_kernel, blk=blk, n_blocks=seq // blk),
        grid=(bsz, n_pairs),
        in_specs=[
            pl.BlockSpec((None, 2, seq, LANES), lambda b, h: (b, h, 0, 0)),
            pl.BlockSpec((None, 2, seq, LANES), lambda b, h: (b, h, 0, 0)),
            pl.BlockSpec((None, None, seq, LANES), lambda b, h: (b, h, 0, 0)),
        ],
        out_specs=pl.BlockSpec((seq, LANES), lambda b, h: (0, b * n_pairs + h)),
        out_shape=jax.ShapeDtypeStruct((seq, bsz * D_MODEL), BF16),
        compiler_params=_params(("parallel", "parallel"), est),
        name="mla_attention",
    )(q, k, v)


def _final_norm_kernel(x_ref, g_ref, o_ref):
    o_ref[...] = _rms(x_ref[...], g_ref[...])


def final_norm(x, g, bsz, seq):
    d = D_MODEL
    tl = min(ROW_TILE, seq)
    return pl.pallas_call(
        _final_norm_kernel,
        grid=(seq // tl, bsz),
        in_specs=[_x_spec(False, tl, d), pl.BlockSpec((1, d), lambda i, b: (0, 0))],
        out_specs=pl.BlockSpec((None, tl, d), lambda i, b: (b, i, 0)),
        out_shape=jax.ShapeDtypeStruct((bsz, seq, d), F32),
        compiler_params=_params(("parallel", "parallel"), 8 * tl * d * 4),
        name="final_norm",
    )(x, g)


def _cast_kernel(x_ref, o_ref):
    o_ref[...] = x_ref[...].astype(o_ref.dtype)


def layer_weight_bf16(w_stacked, layer):
    shape = w_stacked.shape[1:]
    n = shape[-1]
    rows = math.prod(shape[:-1])
    tr = 1 << ((CAST_BLOCK_BYTES // (4 * n)).bit_length() - 1)
    while rows % tr:
        tr //= 2
    out = pl.pallas_call(
        _cast_kernel,
        grid=(rows // tr,),
        in_specs=[pl.BlockSpec((None, tr, n), lambda i: (layer, i, 0))],
        out_specs=pl.BlockSpec((tr, n), lambda i: (i, 0)),
        out_shape=jax.ShapeDtypeStruct((rows, n), BF16),
        compiler_params=_params(("parallel",), 2 * tr * n * 6),
        name="weight_bf16",
    )(w_stacked.reshape(w_stacked.shape[0], rows, n))
    return out.reshape(shape)


def _row(v):
    return v.reshape(1, -1).astype(F32)


def _sb_qkv_weight(w_qkv):
    scale = SB_HEAD_DIM ** -0.5 * LOG2_E
    col_scale = jnp.concatenate([jnp.full((D_MODEL,), scale, F32), jnp.ones((2 * D_MODEL,), F32)])
    return (w_qkv * col_scale[None, :]).astype(BF16)


def _s5_block_diag(b_re, b_im, c_re, c_im):
    per = S5_SLAB // S5_GROUP
    n_slabs = S5_GROUPS // per
    eye = jnp.eye(per, dtype=F32)

    def b_blk(b):
        b4 = b.reshape(n_slabs, per, S5_STATE, S5_GROUP)
        return jnp.einsum('sgpc,gh->sgchp', b4, eye).reshape(n_slabs, per * S5_GROUP, per * S5_STATE)

    def c_blk(c):
        c4 = c.reshape(n_slabs, per, S5_GROUP, S5_STATE)
        return jnp.einsum('sgcp,gh->sgphc', c4, eye).reshape(n_slabs, per * S5_STATE, per * S5_GROUP)

    return b_blk(b_re), b_blk(b_im), c_blk(c_re).astype(BF16), c_blk(c_im).astype(BF16)


def _rotate_half_cols(w):
    half = w.shape[-1] // 2
    return jnp.concatenate([-w[..., half:], w[..., :half]], axis=-1)


def _mla_weights(w_in, w_q_b, w_kv_b):
    base = MLA_Q_RANK + MLA_KV_RANK
    w_kr = w_in[:, base:]
    pad = jnp.zeros((D_MODEL, LANES - MLA_ROPE), F32)
    w_in2 = jnp.concatenate([w_in[:, :base], w_kr, pad, _rotate_half_cols(w_kr), pad], axis=-1).astype(BF16)
    wq = w_q_b.reshape(MLA_Q_RANK, MLA_HEADS, MLA_NOPE + MLA_ROPE)
    q_nope, q_rope = wq[..., :MLA_NOPE], wq[..., MLA_NOPE:]
    zq = jnp.zeros((MLA_Q_RANK, MLA_HEADS, MLA_ROPE), F32)
    wa = jnp.concatenate([q_rope, zq, q_nope], axis=-1).reshape(MLA_Q_RANK, MLA_HEADS * LANES).astype(BF16)
    wb = jnp.concatenate([_rotate_half_cols(q_rope), zq, jnp.zeros_like(q_nope)], axis=-1)
    wb = wb.reshape(MLA_Q_RANK, MLA_HEADS * LANES).astype(BF16)
    wkv = w_kv_b.reshape(MLA_KV_RANK, MLA_HEADS, MLA_NOPE + MLA_V)
    k_nope, v_w = wkv[..., :MLA_NOPE], wkv[..., MLA_NOPE:]
    zk = jnp.zeros((MLA_KV_RANK, MLA_HEADS, 2 * MLA_ROPE), F32)
    wk_top = jnp.concatenate([zk, k_nope], axis=-1).reshape(MLA_KV_RANK, MLA_HEADS * LANES)
    route = jnp.concatenate([jnp.eye(LANES, MLA_ROPE, dtype=F32), jnp.zeros((LANES, LANES - MLA_ROPE), F32)], axis=-1)
    wk_bot = jnp.tile(route, (1, MLA_HEADS))
    wk = jnp.concatenate([wk_top, wk_bot], axis=0).astype(BF16)
    wv = v_w.reshape(MLA_KV_RANK, MLA_HEADS * MLA_V).astype(BF16)
    return w_in2, wa, wb, wk, wv


def _rope_tables(seq, bsz):
    pos = jnp.arange(seq, dtype=F32)
    inv_freq = ROPE_THETA ** (-jnp.arange(0, MLA_ROPE, 2, dtype=F32) / MLA_ROPE)
    ang = pos[:, None] * inv_freq[None, :]
    zeros = jnp.zeros((seq, LANES - MLA_ROPE), F32)
    cos = jnp.concatenate([jnp.cos(ang), jnp.cos(ang), zeros], axis=-1)
    sin = jnp.concatenate([jnp.sin(ang), jnp.sin(ang), zeros], axis=-1)
    return cos, sin


def moe_layer(flat, g_ffn, w_router, b_router, w_gate_up, w_down):
    w_r = jnp.pad(w_router, ((0, 0), (0, LANES - N_EXPERTS)))
    b_r = jnp.pad(b_router, (0, LANES - N_EXPERTS)).reshape(1, LANES)
    gates, sel = router(flat, g_ffn, w_r, b_r)
    return moe_ffn(flat, g_ffn, w_gate_up, w_down, gates, sel)


def kernel(x, norm_mix, norm_ffn, final_norm_g, sb_w_qkv, sb_w_o, s5_lambda_re, s5_lambda_im, s5_log_dt, s5_b_re,
           s5_b_im, s5_c_re, s5_c_im, s5_d, s5_w_glu, mla_w_in, mla_q_norm, mla_w_q_b, mla_kv_norm, mla_w_kv_b,
           mla_w_o, ffn_w_gate_up, ffn_w_down, moe_w_router, moe_b_router, moe_w_gate_up, moe_w_down):
    bsz, seq, d = x.shape
    t = bsz * seq
    blk = min(ATT_BLOCK, seq)
    tri = (jnp.arange(blk)[:, None] > jnp.arange(blk)[None, :]).astype(BF16)
    cos_tab, sin_tab = _rope_tables(seq, bsz)

    cur = x
    is_bld = True

    def wide(a):
        return a if a.shape[0] == seq else a.reshape(seq, bsz * d)

    for i in range(DEPTH):
        kind, slot = i % N_MIXERS, i // N_MIXERS
        g_mix = _row(norm_mix[i])
        if kind == 0:
            cur = cur if is_bld else wide(cur)
            qkv = norm_proj_blocks(cur, is_bld, bsz, seq, g_mix, _sb_qkv_weight(sb_w_qkv[slot]))
            o = sb_attention(qkv, bsz, seq, tri)
            cur = proj_residual(o, cur, is_bld, seq, bsz, sb_w_o[slot].astype(BF16))
        elif kind == 1:
            b_re_blk, b_im_blk, c_re_blk, c_im_blk = _s5_block_diag(
                s5_b_re[slot], s5_b_im[slot], s5_c_re[slot], s5_c_im[slot])
            log_dt_row = jnp.repeat(s5_log_dt[slot], S5_STATE).reshape(1, -1)
            a_re, a_im, bb_re, bb_im = s5_discretise(
                _row(s5_lambda_re[slot]), _row(s5_lambda_im[slot]), log_dt_row, b_re_blk, b_im_blk)
            if is_bld:
                cur = jnp.transpose(cur, (1, 0, 2))
            flat = cur.reshape(t, d)
            y = s5_mixer(flat, g_mix, a_re, a_im, bb_re, bb_im, c_re_blk, c_im_blk, _row(s5_d[slot]), bsz)
            cur = proj_residual(y, flat, False, t, 1, s5_w_glu[slot].astype(BF16), glu=True)
        else:
            if is_bld:
                cur = jnp.transpose(cur, (1, 0, 2))
            cur = wide(cur)
            w_in2, wa, wb, wk, wv = _mla_weights(mla_w_in[slot], mla_w_q_b[slot], mla_w_kv_b[slot])
            q, k, v = mla_proj(cur, bsz, seq, g_mix, w_in2, _row(mla_q_norm[slot]), _row(mla_kv_norm[slot]),
                               wa, wb, wk, wv, cos_tab, sin_tab)
            o = mla_attention(q, k, v, bsz, seq)
            cur = proj_residual(o, cur, False, seq, bsz, mla_w_o[slot].astype(BF16))
        is_bld = False
        g_ffn = _row(norm_ffn[i])
        if i % 2 == 0:
            cur = ffn(cur, g_ffn, layer_weight_bf16(ffn_w_gate_up, i // 2), layer_weight_bf16(ffn_w_down, i // 2))
        else:
            cur = moe_layer(cur, g_ffn, moe_w_router[i // 2], moe_b_router[i // 2],
                            layer_weight_bf16(moe_w_gate_up, i // 2), layer_weight_bf16(moe_w_down, i // 2))
    return final_norm(wide(cur), _row(final_norm_g), bsz, seq)
```
